```python
import math
import jax, jax.numpy as jnp
from jax import lax
import numpy as np

D_MODEL = 1024
BATCH = 8
SEQ = 8192
DEPTH = 1

SSM_GROUP_SIZE = 16
SSM_GROUPS = 32
SSM_WIDTH = SSM_GROUP_SIZE * SSM_GROUPS
SSM_STATE = 64
SSM_CHUNK = 128
SSM_DT_MIN = 1e-3
SSM_DT_MAX = 1e-1
ATTN_PATTERNS = ((128, 1), (512, 4), (2048, 16))
ATTN_HEADS_PER_GROUP = 4
ATTN_HEAD_DIM = 64
ATTN_HEADS = ATTN_HEADS_PER_GROUP * len(ATTN_PATTERNS)
ATTN_WIDTH = ATTN_HEADS * ATTN_HEAD_DIM
ATTN_OUT_WIDTH = ATTN_HEADS_PER_GROUP * ATTN_HEAD_DIM
MEM_LEN = 256
MEM_HEADS = 4
MEM_HEAD_DIM = 128
MEM_WIDTH = MEM_HEADS * MEM_HEAD_DIM
N_BRANCHES = 3
D_FF = 4 * D_MODEL
IN_SPLITS = (SSM_WIDTH, ATTN_WIDTH, ATTN_WIDTH, ATTN_WIDTH, MEM_WIDTH, N_BRANCHES * D_MODEL)
IN_WIDTH = sum(IN_SPLITS)
IN_OFFSETS = tuple(int(o) for o in np.cumsum(IN_SPLITS)[:-1])
RMS_EPS = 1e-6
NEG_INF = -1e30

kernel_name = "hybrid_s5_dilated_attn_memory_gated_block"


def rms_norm(x, g):
    xf = x.astype(jnp.float32)
    y = xf * lax.rsqrt(jnp.mean(xf * xf, axis=-1, keepdims=True) + RMS_EPS)
    return (y * g.astype(jnp.float32)).astype(x.dtype)


def _complex_affine_combine(e1, e2):
    a1r, a1i, b1r, b1i = e1
    a2r, a2i, b2r, b2i = e2
    ar = a2r * a1r - a2i * a1i
    ai = a2r * a1i + a2i * a1r
    br = a2r * b1r - a2i * b1i + b2r
    bi = a2r * b1i + a2i * b1r + b2i
    return ar, ai, br, bi


def s5_ssm(u, lam_re, lam_im, log_dt, b_re, b_im, c_re, c_im, d_skip):
    f32 = jnp.float32
    bsz, l, _ = u.shape
    n_chunks = l // SSM_CHUNK
    u = u.astype(f32).reshape(bsz, n_chunks, SSM_CHUNK, SSM_GROUPS, SSM_GROUP_SIZE)
    u = u.transpose(1, 0, 2, 3, 4)
    lr, li = lam_re.astype(f32), lam_im.astype(f32)
    dt = jnp.exp(log_dt.astype(f32))[:, None]
    mag = jnp.exp(lr * dt)
    a_re, a_im = mag * jnp.cos(li * dt), mag * jnp.sin(li * dt)
    nr, ni = a_re - 1.0, a_im
    den = lr * lr + li * li
    coef_re = (nr * lr + ni * li) / den
    coef_im = (ni * lr - nr * li) / den
    br_, bi_ = b_re.astype(f32), b_im.astype(f32)
    bb_re = coef_re[..., None] * br_ - coef_im[..., None] * bi_
    bb_im = coef_re[..., None] * bi_ + coef_im[..., None] * br_
    cr, ci = c_re.astype(f32), c_im.astype(f32)
    dd = d_skip.astype(f32)

    def chunk_step(carry, u_c):
        s_re0, s_im0 = carry
        bu_re = jnp.einsum('bcgh,gph->bcgp', u_c, bb_re)
        bu_im = jnp.einsum('bcgh,gph->bcgp', u_c, bb_im)
        ar = jnp.broadcast_to(a_re, bu_re.shape)
        ai = jnp.broadcast_to(a_im, bu_re.shape)
        pr, pi, hr, hi = lax.associative_scan(_complex_affine_combine, (ar, ai, bu_re, bu_im), axis=1)
        s_re = hr + pr * s_re0[:, None] - pi * s_im0[:, None]
        s_im = hi + pr * s_im0[:, None] + pi * s_re0[:, None]
        y = (jnp.einsum('bcgp,ghp->bcgh', s_re, cr)
             - jnp.einsum('bcgp,ghp->bcgh', s_im, ci)
             + dd * u_c)
        return (s_re[:, -1], s_im[:, -1]), y

    init = (jnp.zeros((bsz, SSM_GROUPS, SSM_STATE), f32), jnp.zeros((bsz, SSM_GROUPS, SSM_STATE), f32))
    _, y = lax.scan(chunk_step, init, u)
    return y.transpose(1, 0, 2, 3, 4).reshape(bsz, l, SSM_WIDTH)


def dilated_window_attention(q, k, v, window, dilation):
    f32 = jnp.float32
    b, l, h, e = q.shape
    w = window // dilation
    m = l // dilation
    nb = -(-m // w)
    mp = nb * w

    def to_sub(t):
        t = t.reshape(b, m, dilation, h, e).transpose(0, 2, 3, 1, 4)
        t = jnp.pad(t, ((0, 0), (0, 0), (0, 0), (0, mp - m), (0, 0)))
        return t.reshape(b, dilation, h, nb, w, e)

    def with_prev(t):
        prev = jnp.pad(t, ((0, 0), (0, 0), (0, 0), (1, 0), (0, 0), (0, 0)))[:, :, :, :-1]
        return jnp.concatenate([prev, t], axis=4)

    qs, ks, vs = to_sub(q), to_sub(k), to_sub(v)
    kb, vb = with_prev(ks), with_prev(vs)
    s = jnp.einsum('bdhnqe,bdhnke->bdhnqk', qs, kb).astype(f32) * (e ** -0.5)
    qi = jnp.arange(w)[:, None]
    kj = jnp.arange(2 * w)[None, :]
    dist = w + qi - kj
    blk = jnp.arange(nb)[:, None, None]
    valid = (dist >= 0) & (dist <= w) & ((blk > 0) | (kj >= w))
    s = jnp.where(valid, s, NEG_INF)
    mx = jnp.max(s, axis=-1, keepdims=True)
    p = jnp.exp(s - mx)
    den = jnp.sum(p, axis=-1)
    o = jnp.einsum('bdhnqk,bdhnke->bdhnqe', p, vb.astype(f32)) / den[..., None]
    lse = mx[..., 0] + jnp.log(den)
    o = o.reshape(b, dilation, h, mp, e)[:, :, :, :m].transpose(0, 3, 1, 2, 4).reshape(b, l, h, e)
    lse = lse.reshape(b, dilation, h, mp)[..., :m].transpose(0, 3, 1, 2).reshape(b, l, h)
    return o, lse


def memory_cross_attention(q, k, v):
    b, l, hm, e = q.shape
    s = jnp.einsum('blhe,bmhe->bhlm', q, k).astype(jnp.float32) * (e ** -0.5)
    p = jax.nn.softmax(s, axis=-1)
    o = jnp.einsum('bhlm,bmhe->blhe', p, v.astype(jnp.float32))
    return o.astype(q.dtype).reshape(b, l, hm * e)


def setup_inputs(seed: int = 0) -> dict:
    key = jax.random.key(seed)
    ks = jax.random.split(key, 26)
    f32 = jnp.float32

    def nrm(k, shape, scale):
        return jax.random.normal(k, shape, f32) * scale

    G, P, H = SSM_GROUPS, SSM_STATE, SSM_GROUP_SIZE
    return {
        "x": nrm(ks[0], (BATCH, SEQ, D_MODEL), 1.0),
        "mem": nrm(ks[1], (BATCH, MEM_LEN, D_MODEL), 1.0),
        "norm1_g": 1.0 + nrm(ks[2], (DEPTH, D_MODEL), 0.02),
        "mem_norm_g": 1.0 + nrm(ks[3], (DEPTH, D_MODEL), 0.02),
        "w_in": nrm(ks[4], (DEPTH, D_MODEL, IN_WIDTH), D_MODEL ** -0.5),
        "b_gate": nrm(ks[5], (DEPTH, N_BRANCHES * D_MODEL), 0.02),
        "ssm_lambda_re": -0.5 + nrm(ks[6], (DEPTH, G, P), 0.01),
        "ssm_lambda_im": math.pi * jnp.arange(P, dtype=f32) + nrm(ks[7], (DEPTH, G, P), 0.01),
        "ssm_log_dt": jax.random.uniform(ks[8], (DEPTH, G), f32, math.log(SSM_DT_MIN), math.log(SSM_DT_MAX)),
        "ssm_b_re": nrm(ks[9], (DEPTH, G, P, H), (2 * H) ** -0.5),
        "ssm_b_im": nrm(ks[10], (DEPTH, G, P, H), (2 * H) ** -0.5),
        "ssm_c_re": nrm(ks[11], (DEPTH, G, H, P), P ** -0.5),
        "ssm_c_im": nrm(ks[12], (DEPTH, G, H, P), P ** -0.5),
        "ssm_d": nrm(ks[13], (DEPTH, G, H), 1.0),
        "w_glu": nrm(ks[14], (DEPTH, SSM_WIDTH, SSM_WIDTH), SSM_WIDTH ** -0.5),
        "b_glu": nrm(ks[15], (DEPTH, SSM_WIDTH), 0.02),
        "w_ssm_br": nrm(ks[16], (DEPTH, SSM_WIDTH, D_MODEL), SSM_WIDTH ** -0.5),
        "w_attn_br": nrm(ks[17], (DEPTH, ATTN_OUT_WIDTH, D_MODEL), ATTN_OUT_WIDTH ** -0.5),
        "w_mem_kv": nrm(ks[18], (DEPTH, D_MODEL, 2 * MEM_WIDTH), D_MODEL ** -0.5),
        "w_mem_br": nrm(ks[19], (DEPTH, MEM_WIDTH, D_MODEL), MEM_WIDTH ** -0.5),
        "w_o": nrm(ks[20], (DEPTH, D_MODEL, D_MODEL), D_MODEL ** -0.5),
        "norm2_g": 1.0 + nrm(ks[21], (DEPTH, D_MODEL), 0.02),
        "w_up": nrm(ks[22], (DEPTH, D_MODEL, D_FF), D_MODEL ** -0.5),
        "w_down": nrm(ks[23], (DEPTH, D_FF, D_MODEL), D_FF ** -0.5),
        "final_g": 1.0 + nrm(ks[24], (D_MODEL,), 0.02),
    }


def reference(x, mem, norm1_g, mem_norm_g, w_in, b_gate, ssm_lambda_re, ssm_lambda_im, ssm_log_dt,
              ssm_b_re, ssm_b_im, ssm_c_re, ssm_c_im, ssm_d, w_glu, b_glu, w_ssm_br, w_attn_br,
              w_mem_kv, w_mem_br, w_o, norm2_g, w_up, w_down, final_g):
    bsz, seq, _ = x.shape
    h = x
    for i in range(DEPTH):
        n = rms_norm(h, norm1_g[i])
        z = n @ w_in[i]
        u, q, k, v, mq, zg = jnp.split(z, IN_OFFSETS, axis=-1)
        gates = jax.nn.sigmoid(zg + b_gate[i]).reshape(bsz, seq, N_BRANCHES, D_MODEL)

        y = s5_ssm(u, ssm_lambda_re[i], ssm_lambda_im[i], ssm_log_dt[i], ssm_b_re[i], ssm_b_im[i],
                   ssm_c_re[i], ssm_c_im[i], ssm_d[i]).astype(x.dtype)
        y = jax.nn.gelu(y)
        y = y * jax.nn.sigmoid(y @ w_glu[i] + b_glu[i])
        br_ssm = y @ w_ssm_br[i]

        q = q.reshape(bsz, seq, ATTN_HEADS, ATTN_HEAD_DIM)
        k = k.reshape(bsz, seq, ATTN_HEADS, ATTN_HEAD_DIM)
        v = v.reshape(bsz, seq, ATTN_HEADS, ATTN_HEAD_DIM)
        outs, lses = [], []
        for g, (window, dilation) in enumerate(ATTN_PATTERNS):
            sl = slice(g * ATTN_HEADS_PER_GROUP, (g + 1) * ATTN_HEADS_PER_GROUP)
            o_g, lse_g = dilated_window_attention(q[:, :, sl], k[:, :, sl], v[:, :, sl], window, dilation)
            outs.append(o_g)
            lses.append(lse_g)
        wts = jax.nn.softmax(jnp.stack(lses, axis=0), axis=0)
        o = jnp.sum(wts[..., None] * jnp.stack(outs, axis=0), axis=0)
        br_attn = o.astype(x.dtype).reshape(bsz, seq, ATTN_OUT_WIDTH) @ w_attn_br[i]

        kv = rms_norm(mem, mem_norm_g[i]) @ w_mem_kv[i]
        mk, mv = jnp.split(kv, 2, axis=-1)
        mk = mk.reshape(bsz, MEM_LEN, MEM_HEADS, MEM_HEAD_DIM)
        mv = mv.reshape(bsz, MEM_LEN, MEM_HEADS, MEM_HEAD_DIM)
        mq = mq.reshape(bsz, seq, MEM_HEADS, MEM_HEAD_DIM)
        br_mem = memory_cross_attention(mq, mk, mv) @ w_mem_br[i]

        merged = gates[:, :, 0] * br_ssm + gates[:, :, 1] * br_attn + gates[:, :, 2] * br_mem
        h = h + merged @ w_o[i]

        n2 = rms_norm(h, norm2_g[i])
        h = h + jnp.square(jax.nn.relu(n2 @ w_up[i])) @ w_down[i]
    return rms_norm(h, final_g)
```

```python
import functools
import math

import jax
import jax.numpy as jnp
from jax import lax
from jax.experimental import pallas as pl
from jax.experimental.pallas import tpu as pltpu

D_MODEL = 1024
SSM_GROUP_SIZE = 16
SSM_GROUPS = 32
SSM_WIDTH = SSM_GROUP_SIZE * SSM_GROUPS
SSM_STATE = 64
SSM_CHUNK = 128
ATTN_PATTERNS = ((128, 1), (512, 4), (2048, 16))
ATTN_HEADS_PER_GROUP = 4
ATTN_HEAD_DIM = 64
ATTN_GROUP_WIDTH = ATTN_HEADS_PER_GROUP * ATTN_HEAD_DIM
ATTN_WIDTH = ATTN_GROUP_WIDTH * len(ATTN_PATTERNS)
ATTN_BLOCK = 128
MEM_HEADS = 4
MEM_HEAD_DIM = 128
MEM_WIDTH = MEM_HEADS * MEM_HEAD_DIM
N_BRANCHES = 3
D_FF = 4 * D_MODEL
RMS_EPS = 1e-6
NEG_INF = -1e30

VMEM_LIMIT_BYTES = 56 * 1024 * 1024

F32 = jnp.float32
BF16 = jnp.bfloat16


def _params(n_axes):
    return pltpu.CompilerParams(
        dimension_semantics=("arbitrary",) * n_axes, vmem_limit_bytes=VMEM_LIMIT_BYTES)


def _resident(shape):
    nd = len(shape)
    return pl.BlockSpec(shape, lambda *_: (0,) * nd, pipeline_mode=pl.Buffered(1))


def _rms(x, g):
    ms = jnp.mean(x * x, axis=-1, keepdims=True)
    return x * lax.rsqrt(ms + RMS_EPS) * g


def _sigmoid(x):
    return 1.0 / (1.0 + jnp.exp(-x))


def _gelu_tanh(x):
    c = math.sqrt(2.0 / math.pi)
    return x * (0.5 * (1.0 + jnp.tanh(c * (x + 0.044715 * (x * x * x)))))


def _dot(a, b):
    return jnp.dot(a, b, preferred_element_type=F32)


def _dot_nt(a, b):
    return lax.dot_general(a, b, (((1,), (1,)), ((), ())), preferred_element_type=F32)


def _in_proj_body(x_ref, g_ref, wut_ref, wqkv_ref, wmq_ref, wg_ref, ut_ref, qkv_ref, mq_ref, zg_ref):
    n = _rms(x_ref[...], g_ref[...]).astype(BF16)
    ut_ref[...] = _dot_nt(wut_ref[...], n).astype(BF16)
    for c in range(0, 3 * ATTN_WIDTH, ATTN_WIDTH):
        qkv_ref[:, c:c + ATTN_WIDTH] = _dot(n, wqkv_ref[:, c:c + ATTN_WIDTH]).astype(BF16)
    mq_ref[...] = _dot(n, wmq_ref[...]).astype(BF16)
    for c in range(0, N_BRANCHES * D_MODEL, D_MODEL):
        zg_ref[:, c:c + D_MODEL] = _dot(n, wg_ref[:, c:c + D_MODEL]).astype(BF16)


def _in_proj(x2, g, w_ut, w_qkv, w_mq, w_g, bsz, seq, tm):
    tokens = bsz * seq
    nt = seq // tm
    row = lambda b, i: (b * nt + i, 0)
    return pl.pallas_call(
        _in_proj_body,
        grid=(bsz, nt),
        in_specs=[
            pl.BlockSpec((tm, D_MODEL), row),
            _resident((1, D_MODEL)),
            _resident(w_ut.shape), _resident(w_qkv.shape), _resident(w_mq.shape), _resident(w_g.shape),
        ],
        out_specs=[
            pl.BlockSpec((None, SSM_WIDTH, tm), lambda b, i: (b, 0, i)),
            pl.BlockSpec((tm, 3 * ATTN_WIDTH), row),
            pl.BlockSpec((tm, MEM_WIDTH), row),
            pl.BlockSpec((tm, N_BRANCHES * D_MODEL), row),
        ],
        out_shape=[
            jax.ShapeDtypeStruct((bsz, SSM_WIDTH, seq), BF16),
            jax.ShapeDtypeStruct((tokens, 3 * ATTN_WIDTH), BF16),
            jax.ShapeDtypeStruct((tokens, MEM_WIDTH), BF16),
            jax.ShapeDtypeStruct((tokens, N_BRANCHES * D_MODEL), BF16),
        ],
        compiler_params=_params(2),
        name="in_proj",
    )(x2, g, w_ut, w_qkv, w_mq, w_g)


def _ssm_body(u_ref, t_ref, m_ref, n_ref, a_ref, y_ref, z_scr, sp_scr):
    bsz, hh, n_chunks, cc = u_ref.shape
    rows = bsz * n_chunks
    half = 2 * SSM_STATE
    x = jnp.concatenate([u_ref[:, h].reshape(rows, cc) for h in range(hh)], axis=1)
    z_scr[...] = _dot(x, m_ref[...]).reshape(bsz, n_chunks, 2 * half)
    ar = a_ref[0:1, :].reshape(1, 1, half)
    ai = a_ref[1:2, :].reshape(1, 1, half)

    sr = jnp.zeros((bsz, 1, half), F32)
    si = jnp.zeros((bsz, 1, half), F32)
    for c in range(n_chunks):
        sp_scr[:, c:c + 1, 0:half] = sr
        sp_scr[:, c:c + 1, half:2 * half] = si
        zr = z_scr[:, c:c + 1, 0:half]
        zi = z_scr[:, c:c + 1, half:2 * half]
        sr, si = ar * sr - ai * si + zr, ar * si + ai * sr + zi
    sp = sp_scr[...].reshape(rows, 2 * half).astype(BF16)
    y = _dot(x, t_ref[...]) + _dot(sp, n_ref[...])
    for h in range(hh):
        y_ref[:, h] = y[:, h * cc:(h + 1) * cc].reshape(bsz, n_chunks, cc)


def _ssm(u_t, t_mat, m_mat, n_mat, a_pow, bsz, seq):
    n_chunks = seq // SSM_CHUNK
    width = SSM_GROUP_SIZE * SSM_CHUNK
    u5 = u_t.reshape(bsz, SSM_GROUPS, SSM_GROUP_SIZE, n_chunks, SSM_CHUNK)
    blk = (bsz, None, SSM_GROUP_SIZE, n_chunks, SSM_CHUNK)
    y5 = pl.pallas_call(
        _ssm_body,
        grid=(SSM_GROUPS,),
        in_specs=[
            pl.BlockSpec(blk, lambda g: (0, g, 0, 0, 0)),
            pl.BlockSpec((None, width, width), lambda g: (g, 0, 0)),
            pl.BlockSpec((None, width, 4 * SSM_STATE), lambda g: (g, 0, 0)),
            pl.BlockSpec((None, 4 * SSM_STATE, width), lambda g: (g, 0, 0)),
            pl.BlockSpec((None, 8, 2 * SSM_STATE), lambda g: (g, 0, 0)),
        ],
        out_specs=pl.BlockSpec(blk, lambda g: (0, g, 0, 0, 0)),
        out_shape=jax.ShapeDtypeStruct(u5.shape, F32),
        scratch_shapes=[
            pltpu.VMEM((bsz, n_chunks, 4 * SSM_STATE), F32),
            pltpu.VMEM((bsz, n_chunks, 4 * SSM_STATE), F32),
        ],
        compiler_params=_params(1),
        name="ssm",
    )(u5, t_mat, m_mat, n_mat, a_pow)
    return y5.reshape(bsz, SSM_WIDTH, seq)


def _ssm_operators(lam_re, lam_im, log_dt, b_re, b_im, c_re, c_im, d_skip):
    hp = lax.Precision.HIGHEST
    cc, pp, hh, gg = SSM_CHUNK, SSM_STATE, SSM_GROUP_SIZE, SSM_GROUPS
    lr, li = lam_re.astype(F32), lam_im.astype(F32)
    dt = jnp.exp(log_dt.astype(F32))[:, None]
    mag = jnp.exp(lr * dt)
    a_re, a_im = mag * jnp.cos(li * dt), mag * jnp.sin(li * dt)
    nr, ni = a_re - 1.0, a_im
    den = lr * lr + li * li
    coef_re = (nr * lr + ni * li) / den
    coef_im = (ni * lr - nr * li) / den
    br_, bi_ = b_re.astype(F32), b_im.astype(F32)
    bb_re = coef_re[..., None] * br_ - coef_im[..., None] * bi_
    bb_im = coef_re[..., None] * bi_ + coef_im[..., None] * br_
    cr, ci = c_re.astype(F32), c_im.astype(F32)
    dd = d_skip.astype(F32)

    k = jnp.arange(cc + 1, dtype=F32)[:, None, None]
    pmag = jnp.exp(k * (lr * dt)[None])
    pang = k * (li * dt)[None]
    p_re, p_im = pmag * jnp.cos(pang), pmag * jnp.sin(pang)

    cp_re = cr[None] * p_re[:cc, :, None, :] - ci[None] * p_im[:cc, :, None, :]
    cp_im = cr[None] * p_im[:cc, :, None, :] + ci[None] * p_re[:cc, :, None, :]
    kern = (jnp.einsum('kgop,gph->kgoh', cp_re, bb_re, precision=hp)
            - jnp.einsum('kgop,gph->kgoh', cp_im, bb_im, precision=hp))
    kern = kern.at[0].add(dd[:, :, None] * jnp.eye(hh, dtype=F32)[None])
    s_idx = jnp.arange(cc)[:, None]
    t_idx = jnp.arange(cc)[None, :]
    lag = t_idx - s_idx
    kg = kern.transpose(1, 3, 2, 0)
    toep = jnp.take(kg, jnp.clip(lag, 0, cc - 1), axis=3)
    toep = jnp.where(lag >= 0, toep, 0.0).transpose(0, 1, 3, 2, 4)
    t_mat = toep.reshape(gg, hh * cc, hh * cc).astype(BF16)

    q_re, q_im = p_re[cc - 1 - jnp.arange(cc)], p_im[cc - 1 - jnp.arange(cc)]
    m_re = q_re[:, :, :, None] * bb_re[None] - q_im[:, :, :, None] * bb_im[None]
    m_im = q_re[:, :, :, None] * bb_im[None] + q_im[:, :, :, None] * bb_re[None]
    m_re = m_re.transpose(1, 3, 0, 2).reshape(gg, hh * cc, pp)
    m_im = m_im.transpose(1, 3, 0, 2).reshape(gg, hh * cc, pp)
    zpad = jnp.zeros_like(m_re)
    m_mat = jnp.concatenate([m_re, zpad, m_im, zpad], axis=-1).astype(BF16)

    r_re, r_im = p_re[1:cc + 1], p_im[1:cc + 1]
    n_re = cr[None] * r_re[:, :, None, :] - ci[None] * r_im[:, :, None, :]
    n_im = -(cr[None] * r_im[:, :, None, :] + ci[None] * r_re[:, :, None, :])
    n_re = n_re.transpose(1, 3, 2, 0).reshape(gg, pp, hh * cc)
    n_im = n_im.transpose(1, 3, 2, 0).reshape(gg, pp, hh * cc)
    zpad = jnp.zeros_like(n_re)
    n_mat = jnp.concatenate([n_re, zpad, n_im, zpad], axis=1).astype(BF16)

    zp = jnp.zeros((gg, pp), F32)
    a_pow = jnp.zeros((gg, 8, 2 * pp), F32)
    a_pow = a_pow.at[:, 0].set(jnp.concatenate([p_re[cc], zp], axis=-1))
    a_pow = a_pow.at[:, 1].set(jnp.concatenate([p_im[cc], zp], axis=-1))
    return t_mat, m_mat, n_mat, a_pow


def _dil_attn_body(q_ref, kc_ref, kh_ref, vc_ref, vh_ref, o_ref, l_ref, *, n_blk):
    i = pl.program_id(2)
    w = ATTN_BLOCK
    e = ATTN_HEAD_DIM
    row = lax.broadcasted_iota(jnp.int32, (w, w), 0)
    col = lax.broadcasted_iota(jnp.int32, (w, w), 1)
    mask_prev = col >= row
    mask_cur = col <= row
    scale = e ** -0.5
    for j in range(n_blk):
        rs = slice(j * w, (j + 1) * w)
        q = q_ref[rs, :]
        kc = kc_ref[rs, :]
        vc = vc_ref[rs, :]
        if j == 0:
            kp, vp = kh_ref[...], vh_ref[...]
            mp = jnp.logical_and(mask_prev, i > 0)
        else:
            ps = slice((j - 1) * w, j * w)
            kp, vp = kc_ref[ps, :], vc_ref[ps, :]
            mp = mask_prev
        outs, lses = [], []
        for h in range(ATTN_HEADS_PER_GROUP):
            sl = slice(h * e, (h + 1) * e)
            qh = q[:, sl]
            s_p = jnp.where(mp, _dot_nt(qh, kp[:, sl]) * scale, NEG_INF)
            s_c = jnp.where(mask_cur, _dot_nt(qh, kc[:, sl]) * scale, NEG_INF)
            mx = jnp.maximum(jnp.max(s_p, axis=-1, keepdims=True), jnp.max(s_c, axis=-1, keepdims=True))
            p_p = jnp.exp(s_p - mx)
            p_c = jnp.exp(s_c - mx)
            den = jnp.sum(p_p, axis=-1, keepdims=True) + jnp.sum(p_c, axis=-1, keepdims=True)
            o = (_dot(p_p.astype(BF16), vp[:, sl]) + _dot(p_c.astype(BF16), vc[:, sl])) / den
            outs.append(o)
            lses.append(jnp.broadcast_to(mx + jnp.log(den), (w, e)))
        o_ref[rs, :] = jnp.concatenate(outs, axis=1).astype(BF16)
        l_ref[rs, :] = jnp.concatenate(lses, axis=1)


def _dil_attn(qkv, group, dilation, bsz, seq, n_blk):
    m = seq // dilation
    tq = n_blk * ATTN_BLOCK
    gw = ATTN_GROUP_WIDTH
    per_tok = 3 * ATTN_WIDTH // gw
    qkv3 = qkv.reshape(bsz, m, dilation * 3 * ATTN_WIDTH)
    q_col, k_col, v_col = group, ATTN_WIDTH // gw + group, 2 * ATTN_WIDTH // gw + group

    def cur(c0):
        return pl.BlockSpec((None, tq, gw), lambda b, r, i: (b, i, per_tok * r + c0))

    def halo(c0):
        return pl.BlockSpec((None, ATTN_BLOCK, gw),
                            lambda b, r, i: (b, jnp.maximum(i * n_blk - 1, 0), per_tok * r + c0))

    out_spec = pl.BlockSpec((None, tq, gw), lambda b, r, i: (b, i, r))
    o, lse = pl.pallas_call(
        functools.partial(_dil_attn_body, n_blk=n_blk),
        grid=(bsz, dilation, m // tq),
        in_specs=[cur(q_col), cur(k_col), halo(k_col), cur(v_col), halo(v_col)],
        out_specs=[out_spec, out_spec],
        out_shape=[jax.ShapeDtypeStruct((bsz, m, dilation * gw), BF16),
                   jax.ShapeDtypeStruct((bsz, m, dilation * gw), F32)],
        compiler_params=_params(3),
        name=f"dil_attn_d{dilation}",
    )(qkv3, qkv3, qkv3, qkv3, qkv3)
    return o.reshape(bsz * seq, gw), lse.reshape(bsz * seq, gw)


def _mem_kv_body(mem_ref, g_ref, w_ref, k_ref, v_ref):
    n = _rms(mem_ref[...], g_ref[...]).astype(BF16)
    kv = _dot(n, w_ref[...])
    k_ref[...] = kv[:, :MEM_WIDTH].astype(BF16)
    v_ref[...] = kv[:, MEM_WIDTH:].astype(BF16)


def _mem_kv(mem, g, w_kv):
    bsz, mem_len, _ = mem.shape
    blk = pl.BlockSpec((None, mem_len, MEM_WIDTH), lambda b: (b, 0, 0))
    return pl.pallas_call(
        _mem_kv_body,
        grid=(bsz,),
        in_specs=[pl.BlockSpec((None, mem_len, D_MODEL), lambda b: (b, 0, 0)),
                  _resident((1, D_MODEL)), _resident(w_kv.shape)],
        out_specs=[blk, blk],
        out_shape=[jax.ShapeDtypeStruct((bsz, mem_len, MEM_WIDTH), BF16)] * 2,
        compiler_params=_params(1),
        name="mem_kv",
    )(mem, g, w_kv)


def _merge_body(yt_ref, o0_ref, o1_ref, o2_ref, l0_ref, l1_ref, l2_ref, mq_ref, mk_ref, mv_ref,
                zg_ref, x_ref, wglut_ref, bglu_ref, wssm_ref, wattn_ref, wmem_ref, wo_ref, bgate_ref,
                h_ref):
    gy = _gelu_tanh(yt_ref[...])
    glu = _dot(wglut_ref[...], gy.astype(BF16)) + bglu_ref[...]
    y2 = (gy * _sigmoid(glu)).T.astype(BF16)
    br_ssm = _dot(y2, wssm_ref[...])

    l0, l1, l2 = l0_ref[...], l1_ref[...], l2_ref[...]
    mx = jnp.maximum(jnp.maximum(l0, l1), l2)
    e0, e1, e2 = jnp.exp(l0 - mx), jnp.exp(l1 - mx), jnp.exp(l2 - mx)
    o = (e0 * o0_ref[...].astype(F32) + e1 * o1_ref[...].astype(F32)
         + e2 * o2_ref[...].astype(F32)) / (e0 + e1 + e2)
    br_attn = _dot(o.astype(BF16), wattn_ref[...])

    e = MEM_HEAD_DIM
    scale = e ** -0.5
    heads = []
    for h in range(MEM_HEADS):
        sl = slice(h * e, (h + 1) * e)
        s = _dot_nt(mq_ref[:, sl], mk_ref[:, sl]) * scale
        p = jnp.exp(s - jnp.max(s, axis=-1, keepdims=True))
        p = p / jnp.sum(p, axis=-1, keepdims=True)
        heads.append(_dot(p.astype(BF16), mv_ref[:, sl]))
    br_mem = _dot(jnp.concatenate(heads, axis=1).astype(BF16), wmem_ref[...])

    d = D_MODEL
    merged = (_sigmoid(zg_ref[:, 0:d].astype(F32) + bgate_ref[:, 0:d]) * br_ssm
              + _sigmoid(zg_ref[:, d:2 * d].astype(F32) + bgate_ref[:, d:2 * d]) * br_attn
              + _sigmoid(zg_ref[:, 2 * d:3 * d].astype(F32) + bgate_ref[:, 2 * d:3 * d]) * br_mem)
    h_ref[...] = x_ref[...] + _dot(merged.astype(BF16), wo_ref[...])


def _merge(y_t, os, ls, mq, mk, mv, zg, x2, w_glu_t, b_glu, w_ssm, w_attn, w_mem, w_o, b_gate,
           bsz, seq, tm):
    nt = seq // tm
    mem_len = mk.shape[1]
    row = lambda b, i: (b * nt + i, 0)
    tok = lambda width: pl.BlockSpec((tm, width), row)
    per_batch = pl.BlockSpec((None, mem_len, MEM_WIDTH), lambda b, i: (b, 0, 0))
    weights = [w_glu_t, b_glu, w_ssm, w_attn, w_mem, w_o, b_gate]
    return pl.pallas_call(
        _merge_body,
        grid=(bsz, nt),
        in_specs=[pl.BlockSpec((None, SSM_WIDTH, tm), lambda b, i: (b, 0, i))]
        + [tok(ATTN_GROUP_WIDTH)] * 6
        + [tok(MEM_WIDTH), per_batch, per_batch, tok(N_BRANCHES * D_MODEL), tok(D_MODEL)]
        + [_resident(w.shape) for w in weights],
        out_specs=tok(D_MODEL),
        out_shape=jax.ShapeDtypeStruct((bsz * seq, D_MODEL), F32),
        compiler_params=_params(2),
        name="merge",
    )(y_t, *os, *ls, mq, mk, mv, zg, x2, *weights)


def _mlp_body(h_ref, g2_ref, wup_ref, wdown_ref, gf_ref, out_ref, *, final_norm, ff_chunk):
    h = h_ref[...]
    n2 = _rms(h, g2_ref[...]).astype(BF16)
    acc = h
    for c in range(0, D_FF, ff_chunk):
        a = jnp.maximum(_dot(n2, wup_ref[:, c:c + ff_chunk]), 0.0)
        acc = acc + _dot((a * a).astype(BF16), wdown_ref[c:c + ff_chunk, :])
    out_ref[...] = _rms(acc, gf_ref[...]) if final_norm else acc


def _mlp(h, g2, w_up, w_down, gf, final_norm, tm):
    tokens = h.shape[0]
    row = lambda i: (i, 0)
    return pl.pallas_call(
        functools.partial(_mlp_body, final_norm=final_norm, ff_chunk=D_MODEL),
        grid=(tokens // tm,),
        in_specs=[pl.BlockSpec((tm, D_MODEL), row), _resident((1, D_MODEL)),
                  _resident(w_up.shape), _resident(w_down.shape), _resident((1, D_MODEL))],
        out_specs=pl.BlockSpec((tm, D_MODEL), row),
        out_shape=jax.ShapeDtypeStruct((tokens, D_MODEL), F32),
        compiler_params=_params(1),
        name="mlp",
    )(h, g2, w_up, w_down, gf)


def _token_tile(seq):
    return 512 if seq % 512 == 0 else ATTN_BLOCK


def kernel(x, mem, norm1_g, mem_norm_g, w_in, b_gate, ssm_lambda_re, ssm_lambda_im, ssm_log_dt, ssm_b_re, ssm_b_im, ssm_c_re, ssm_c_im, ssm_d, w_glu, b_glu, w_ssm_br, w_attn_br, w_mem_kv, w_mem_br, w_o, norm2_g, w_up, w_down, final_g):
    bsz, seq, d_model = x.shape
    depth = w_in.shape[0]
    assert d_model == D_MODEL
    assert all(seq % (ATTN_BLOCK * d) == 0 for _, d in ATTN_PATTERNS) and seq % SSM_CHUNK == 0
    tm = _token_tile(seq)
    h = x.reshape(bsz * seq, D_MODEL)
    row = lambda v: v.reshape(1, -1).astype(F32)
    o_u, o_q, o_mq = 0, SSM_WIDTH, SSM_WIDTH + 3 * ATTN_WIDTH
    o_g = o_mq + MEM_WIDTH
    for i in range(depth):
        wi = w_in[i]
        u_t, qkv, mq, zg = _in_proj(
            h, row(norm1_g[i]), wi[:, o_u:o_q].T.astype(BF16), wi[:, o_q:o_mq].astype(BF16),
            wi[:, o_mq:o_g].astype(BF16), wi[:, o_g:].astype(BF16), bsz, seq, tm)

        ops = _ssm_operators(ssm_lambda_re[i], ssm_lambda_im[i], ssm_log_dt[i], ssm_b_re[i],
                             ssm_b_im[i], ssm_c_re[i], ssm_c_im[i], ssm_d[i])
        y_t = _ssm(u_t, *ops, bsz, seq)

        os, ls = [], []
        for g, (window, dilation) in enumerate(ATTN_PATTERNS):
            assert window // dilation == ATTN_BLOCK
            n_blk = min(4, seq // dilation // ATTN_BLOCK)
            o_g_, l_g_ = _dil_attn(qkv, g, dilation, bsz, seq, n_blk)
            os.append(o_g_)
            ls.append(l_g_)

        mk, mv = _mem_kv(mem, row(mem_norm_g[i]), w_mem_kv[i].astype(BF16))

        h = _merge(y_t, os, ls, mq, mk, mv, zg, h,
                   w_glu[i].T.astype(BF16), b_glu[i].reshape(-1, 1).astype(F32),
                   w_ssm_br[i].astype(BF16), w_attn_br[i].astype(BF16), w_mem_br[i].astype(BF16),
                   w_o[i].astype(BF16), row(b_gate[i]), bsz, seq, tm)

        h = _mlp(h, row(norm2_g[i]), w_up[i].astype(BF16), w_down[i].astype(BF16), row(final_g),
                 final_norm=(i == depth - 1), tm=tm)
    return h.reshape(bsz, seq, D_MODEL)
```

```python
import functools
import math

import jax
import jax.numpy as jnp
from jax import lax
from jax.experimental import pallas as pl
from jax.experimental.pallas import tpu as pltpu

D_MODEL = 1024
SSM_GROUP_SIZE = 16
SSM_GROUPS = 32
SSM_WIDTH = SSM_GROUP_SIZE * SSM_GROUPS
SSM_STATE = 64
SSM_CHUNK = 128
ATTN_PATTERNS = ((128, 1), (512, 4), (2048, 16))
ATTN_HEADS_PER_GROUP = 4
ATTN_HEAD_DIM = 64
ATTN_GROUP_WIDTH = ATTN_HEADS_PER_GROUP * ATTN_HEAD_DIM
ATTN_WIDTH = ATTN_GROUP_WIDTH * len(ATTN_PATTERNS)
ATTN_BLOCK = 128
MAX_DILATION = 16
ATTN_TILE = ATTN_BLOCK * MAX_DILATION
ATTN_BATCH = 4
MEM_HEADS = 4
MEM_HEAD_DIM = 128
MEM_WIDTH = MEM_HEADS * MEM_HEAD_DIM
N_BRANCHES = 3
D_FF = 4 * D_MODEL
RMS_EPS = 1e-6
NEG_INF = -1e30

LANES = 128

VMEM_LIMIT_BYTES = 56 * 1024 * 1024

F32 = jnp.float32
BF16 = jnp.bfloat16


def _params(n_axes):
    return pltpu.CompilerParams(
        dimension_semantics=("arbitrary",) * n_axes, vmem_limit_bytes=VMEM_LIMIT_BYTES)


def _resident(shape):
    nd = len(shape)
    return pl.BlockSpec(shape, lambda *_: (0,) * nd, pipeline_mode=pl.Buffered(1))


def _rms(x, g):
    ms = jnp.mean(x * x, axis=-1, keepdims=True)
    return x * lax.rsqrt(ms + RMS_EPS) * g


def _sigmoid(x):
    return 1.0 / (1.0 + jnp.exp(-x))


def _gelu_tanh(x):
    c = math.sqrt(2.0 / math.pi)
    return x * (0.5 * (1.0 + jnp.tanh(c * (x + 0.044715 * (x * x * x)))))


def _dot(a, b):
    return jnp.dot(a, b, preferred_element_type=F32)


def _dot_nt(a, b):
    return lax.dot_general(a, b, (((1,), (1,)), ((), ())), preferred_element_type=F32)


def _in_proj_body(x_ref, g_ref, wut_ref, wqkv0_ref, wqkvd_ref, wmq_ref, wg_ref,
                  u_ref, qkv0_ref, qkvd_ref, mq_ref, zg_ref, n_scr):
    tm = x_ref.shape[0]
    nf = _rms(x_ref[...], g_ref[...])
    n = nf.astype(BF16)
    for c in range(D_MODEL // LANES):
        n_scr[c] = nf[:, c * LANES:(c + 1) * LANES]
    for c in range(tm // SSM_CHUNK):
        ut = _dot_nt(wut_ref[...], n[c * SSM_CHUNK:(c + 1) * SSM_CHUNK])
        u_ref[:, c * SSM_GROUP_SIZE:(c + 1) * SSM_GROUP_SIZE, :] = ut.reshape(
            SSM_GROUPS, SSM_GROUP_SIZE, SSM_CHUNK)
    qkv0_ref[...] = _dot(n, wqkv0_ref[...]).astype(BF16)
    mq_ref[...] = _dot(n, wmq_ref[...]).astype(BF16)
    for c in range(0, N_BRANCHES * D_MODEL, D_MODEL):
        zg_ref[:, c:c + D_MODEL] = _dot(n, wg_ref[:, c:c + D_MODEL]).astype(BF16)
    rows = tm // MAX_DILATION
    n_perm = jnp.concatenate(
        [jnp.concatenate([n_scr[c, pl.ds(r, rows, stride=MAX_DILATION), :]
                          for c in range(D_MODEL // LANES)], axis=1)
         for r in range(MAX_DILATION)], axis=0)
    zp = _dot(n_perm.astype(BF16), wqkvd_ref[...]).astype(BF16)
    for r in range(MAX_DILATION):
        qkvd_ref[r] = zp[r * rows:(r + 1) * rows]


def _in_proj(x2, g, w_ut, w_qkv0, w_qkvd, w_mq, w_g, bsz, seq, tm):
    tokens = bsz * seq
    nt = seq // tm
    cpt = tm // SSM_CHUNK
    n_chunks = seq // SSM_CHUNK
    wd = w_qkvd.shape[1]
    row = lambda b, i: (b * nt + i, 0)
    return pl.pallas_call(
        _in_proj_body,
        grid=(bsz, nt),
        in_specs=[
            pl.BlockSpec((tm, D_MODEL), row),
            _resident((1, D_MODEL)),
            _resident(w_ut.shape), _resident(w_qkv0.shape), _resident(w_qkvd.shape),
            _resident(w_mq.shape), _resident(w_g.shape),
        ],
        out_specs=[
            pl.BlockSpec((SSM_GROUPS, cpt * SSM_GROUP_SIZE, SSM_CHUNK), lambda b, i: (0, b * nt + i, 0)),
            pl.BlockSpec((tm, ATTN_WIDTH), row),
            pl.BlockSpec((None, MAX_DILATION, tm // MAX_DILATION, wd), lambda b, i: (b, 0, i, 0)),
            pl.BlockSpec((tm, MEM_WIDTH), row),
            pl.BlockSpec((tm, N_BRANCHES * D_MODEL), row),
        ],
        out_shape=[
            jax.ShapeDtypeStruct((SSM_GROUPS, bsz * n_chunks * SSM_GROUP_SIZE, SSM_CHUNK), F32),
            jax.ShapeDtypeStruct((tokens, ATTN_WIDTH), BF16),
            jax.ShapeDtypeStruct((bsz, MAX_DILATION, seq // MAX_DILATION, wd), BF16),
            jax.ShapeDtypeStruct((tokens, MEM_WIDTH), BF16),
            jax.ShapeDtypeStruct((tokens, N_BRANCHES * D_MODEL), BF16),
        ],
        scratch_shapes=[pltpu.VMEM((D_MODEL // LANES, tm, LANES), F32)],
        compiler_params=_params(2),
        name="in_proj",
    )(x2, g, w_ut, w_qkv0, w_qkvd, w_mq, w_g)


def _ssm_body(u_ref, k_ref, m_ref, n_ref, a_ref, y_ref, t_scr, z_scr, sp_scr, *, bsz):
    hh = SSM_GROUP_SIZE
    cc = u_ref.shape[1]
    rows = u_ref.shape[0] // hh
    n_chunks = rows // bsz
    half = 2 * SSM_STATE

    srow = lax.broadcasted_iota(jnp.int32, (cc, cc), 0)
    tcol = lax.broadcasted_iota(jnp.int32, (cc, cc), 1)
    causal = tcol >= srow

    def build(h, carry):
        taps = k_ref[pl.ds(pl.multiple_of(h * hh, hh), hh), :]
        for hp in range(hh):
            kb = jnp.broadcast_to(taps[hp:hp + 1, :], (cc, cc))
            blk = pltpu.roll(kb, 0, 1, stride=1, stride_axis=0)
            t_scr[pl.ds(pl.multiple_of(h * cc, cc), cc), hp * cc:(hp + 1) * cc] = (
                jnp.where(causal, blk, 0.0).astype(BF16))
        return carry

    lax.fori_loop(0, hh, build, 0)

    x = jnp.concatenate([u_ref[pl.ds(h, rows, stride=hh), :].astype(BF16) for h in range(hh)],
                        axis=1)
    z_scr[...] = _dot(x, m_ref[...]).reshape(bsz, n_chunks, 2 * half)
    ar = a_ref[0:1, :].reshape(1, 1, half)
    ai = a_ref[1:2, :].reshape(1, 1, half)
    sr = jnp.zeros((bsz, 1, half), F32)
    si = jnp.zeros((bsz, 1, half), F32)
    for c in range(n_chunks):
        sp_scr[:, c:c + 1, 0:half] = sr
        sp_scr[:, c:c + 1, half:2 * half] = si
        zr = z_scr[:, c:c + 1, 0:half]
        zi = z_scr[:, c:c + 1, half:2 * half]
        sr, si = ar * sr - ai * si + zr, ar * si + ai * sr + zi
    sp = sp_scr[...].reshape(rows, 2 * half).astype(BF16)
    y = _dot(x, t_scr[...]) + _dot(sp, n_ref[...])
    for h in range(hh):
        y_ref[pl.ds(h, rows, stride=hh), :] = y[:, h * cc:(h + 1) * cc]


def _ssm(u3, taps, m_mat, n_mat, a_pow, bsz):
    hh = SSM_GROUP_SIZE
    _, rows16, cc = u3.shape
    n_chunks = rows16 // hh // bsz
    width = hh * cc
    blk = pl.BlockSpec((None, rows16, cc), lambda g: (g, 0, 0))
    return pl.pallas_call(
        functools.partial(_ssm_body, bsz=bsz),
        grid=(SSM_GROUPS,),
        in_specs=[
            blk,
            pl.BlockSpec((None, hh * hh, cc), lambda g: (g, 0, 0)),
            pl.BlockSpec((None, width, 4 * SSM_STATE), lambda g: (g, 0, 0)),
            pl.BlockSpec((None, 4 * SSM_STATE, width), lambda g: (g, 0, 0)),
            pl.BlockSpec((None, 8, 2 * SSM_STATE), lambda g: (g, 0, 0)),
        ],
        out_specs=blk,
        out_shape=jax.ShapeDtypeStruct(u3.shape, F32),
        scratch_shapes=[
            pltpu.VMEM((width, width), BF16),
            pltpu.VMEM((bsz, n_chunks, 4 * SSM_STATE), F32),
            pltpu.VMEM((bsz, n_chunks, 4 * SSM_STATE), F32),
        ],
        compiler_params=_params(1),
        name="ssm",
    )(u3, taps, m_mat, n_mat, a_pow)


def _ssm_operators(lam_re, lam_im, log_dt, b_re, b_im, c_re, c_im, d_skip):
    hp = lax.Precision.HIGHEST
    cc, pp, hh, gg = SSM_CHUNK, SSM_STATE, SSM_GROUP_SIZE, SSM_GROUPS
    lr, li = lam_re.astype(F32), lam_im.astype(F32)
    dt = jnp.exp(log_dt.astype(F32))[:, None]
    mag = jnp.exp(lr * dt)
    a_re, a_im = mag * jnp.cos(li * dt), mag * jnp.sin(li * dt)
    nr, ni = a_re - 1.0, a_im
    den = lr * lr + li * li
    coef_re = (nr * lr + ni * li) / den
    coef_im = (ni * lr - nr * li) / den
    br_, bi_ = b_re.astype(F32), b_im.astype(F32)
    bb_re = coef_re[..., None] * br_ - coef_im[..., None] * bi_
    bb_im = coef_re[..., None] * bi_ + coef_im[..., None] * br_
    cr, ci = c_re.astype(F32), c_im.astype(F32)
    dd = d_skip.astype(F32)

    k = jnp.arange(cc + 1, dtype=F32)[:, None, None]
    pmag = jnp.exp(k * (lr * dt)[None])
    pang = k * (li * dt)[None]
    p_re, p_im = pmag * jnp.cos(pang), pmag * jnp.sin(pang)

    cp_re = cr[None] * p_re[:cc, :, None, :] - ci[None] * p_im[:cc, :, None, :]
    cp_im = cr[None] * p_im[:cc, :, None, :] + ci[None] * p_re[:cc, :, None, :]
    kern = (jnp.einsum('kgop,gph->kgoh', cp_re, bb_re, precision=hp)
            - jnp.einsum('kgop,gph->kgoh', cp_im, bb_im, precision=hp))
    kern = kern.at[0].add(dd[:, :, None] * jnp.eye(hh, dtype=F32)[None])
    taps = kern.transpose(1, 3, 2, 0).reshape(gg, hh * hh, cc)

    q_re, q_im = p_re[cc - 1 - jnp.arange(cc)], p_im[cc - 1 - jnp.arange(cc)]
    m_re = q_re[:, :, :, None] * bb_re[None] - q_im[:, :, :, None] * bb_im[None]
    m_im = q_re[:, :, :, None] * bb_im[None] + q_im[:, :, :, None] * bb_re[None]
    m_re = m_re.transpose(1, 3, 0, 2).reshape(gg, hh * cc, pp)
    m_im = m_im.transpose(1, 3, 0, 2).reshape(gg, hh * cc, pp)
    zpad = jnp.zeros_like(m_re)
    m_mat = jnp.concatenate([m_re, zpad, m_im, zpad], axis=-1).astype(BF16)

    r_re, r_im = p_re[1:cc + 1], p_im[1:cc + 1]
    n_re = cr[None] * r_re[:, :, None, :] - ci[None] * r_im[:, :, None, :]
    n_im = -(cr[None] * r_im[:, :, None, :] + ci[None] * r_re[:, :, None, :])
    n_re = n_re.transpose(1, 3, 2, 0).reshape(gg, pp, hh * cc)
    n_im = n_im.transpose(1, 3, 2, 0).reshape(gg, pp, hh * cc)
    zpad = jnp.zeros_like(n_re)
    n_mat = jnp.concatenate([n_re, zpad, n_im, zpad], axis=1).astype(BF16)

    zp = jnp.zeros((gg, pp), F32)
    a_pow = jnp.zeros((gg, 8, 2 * pp), F32)
    a_pow = a_pow.at[:, 0].set(jnp.concatenate([p_re[cc], zp], axis=-1))
    a_pow = a_pow.at[:, 1].set(jnp.concatenate([p_im[cc], zp], axis=-1))
    return taps, m_mat, n_mat, a_pow


def _attn_blocks(blocks, pos_of, s_scr, p_scr):
    w, e, nh = ATTN_BLOCK, ATTN_HEAD_DIM, ATTN_HEADS_PER_GROUP
    n = len(blocks) * nh
    lane = lax.broadcasted_iota(jnp.int32, (w, 2 * w), 1)
    pkey = pos_of(lane & (w - 1))
    pqry = pos_of(lax.broadcasted_iota(jnp.int32, (w, 2 * w), 0))
    mask = jnp.logical_or(jnp.logical_and(lane < w, pkey >= pqry),
                          jnp.logical_and(lane >= w, pkey <= pqry))
    scale = e ** -0.5
    for bi, (q, kcat, _, has_prev, _) in enumerate(blocks):
        qs = q * scale
        for h in range(nh):
            sl = slice(h * e, (h + 1) * e)
            s = _dot_nt(qs[:, sl], kcat[:, sl])
            if has_prev is not None:
                s = jnp.where(jnp.logical_or(lane >= w, has_prev), s, NEG_INF)
            s_scr[bi * nh + h] = s
    s_all = jnp.where(mask[None], s_scr[0:n], NEG_INF)
    mx = jnp.max(s_all, axis=-1, keepdims=True)
    p = jnp.exp(s_all - mx)
    den = jnp.sum(p, axis=-1, keepdims=True)
    p_scr[0:n] = p.astype(BF16)
    lse = mx + jnp.log(den)
    inv = 1.0 / den
    for bi, (_, _, vcat, _, write) in enumerate(blocks):
        outs, lses = [], []
        for h in range(nh):
            sl = slice(h * e, (h + 1) * e)
            outs.append(_dot(p_scr[bi * nh + h], vcat[:, sl]) * inv[bi * nh + h])
            lses.append(jnp.broadcast_to(lse[bi * nh + h], (w, e)))
        write(jnp.concatenate(outs, axis=1), jnp.concatenate(lses, axis=1))


def _dil_attn_nat_body(q_ref, kc_ref, kh_ref, vc_ref, vh_ref, o_ref, l_ref, s_scr, p_scr):
    i = pl.program_id(1)
    w = ATTN_BLOCK
    pos_of = lambda idx: idx
    n_blk = q_ref.shape[0] // w
    for j0 in range(0, n_blk, ATTN_BATCH):
        blocks = []
        for j in range(j0, j0 + ATTN_BATCH):
            rs = slice(j * w, (j + 1) * w)
            if j == 0:
                kcat = jnp.concatenate([kh_ref[...], kc_ref[rs, :]], axis=0)
                vcat = jnp.concatenate([vh_ref[...], vc_ref[rs, :]], axis=0)
                has_prev = i > 0
            else:
                kcat, vcat, has_prev = kc_ref[(j - 1) * w:(j + 1) * w, :], vc_ref[(j - 1) * w:(j + 1) * w, :], None

            def write(o, lse, rs=rs):
                for half in range(ATTN_GROUP_WIDTH // LANES):
                    ls = slice(half * LANES, (half + 1) * LANES)
                    o_ref[half, rs, :] = o[:, ls]
                    l_ref[half, rs, :] = lse[:, ls]

            blocks.append((q_ref[rs, :], kcat, vcat, has_prev, write))
        _attn_blocks(blocks, pos_of, s_scr, p_scr)


def _dil_attn_res_body(q_ref, kc_ref, kh_ref, vc_ref, vh_ref, o_ref, l_ref, s_scr, p_scr, *, dilation):
    i = pl.program_id(1)
    w = ATTN_BLOCK
    nq = MAX_DILATION // dilation
    rr = w // nq
    shift = rr.bit_length() - 1
    pos_of = lambda idx: nq * (idx & (rr - 1)) + (idx >> shift)

    def slabs(ref, r, lo):
        return [ref[q * dilation + r, lo:lo + rr, :] for q in range(nq)]

    todo = [(r, b) for r in range(dilation) for b in range(nq)]
    for j0 in range(0, len(todo), ATTN_BATCH):
        blocks = []
        for r, b in todo[j0:j0 + ATTN_BATCH]:
            if b == 0:
                kprev = [kh_ref[q * dilation + r] for q in range(nq)]
                vprev = [vh_ref[q * dilation + r] for q in range(nq)]
                has_prev = i > 0
            else:
                kprev, vprev, has_prev = slabs(kc_ref, r, (b - 1) * rr), slabs(vc_ref, r, (b - 1) * rr), None
            kcat = jnp.concatenate(kprev + slabs(kc_ref, r, b * rr), axis=0)
            vcat = jnp.concatenate(vprev + slabs(vc_ref, r, b * rr), axis=0)
            q = jnp.concatenate(slabs(q_ref, r, b * rr), axis=0)

            def write(o, lse, r=r, b=b):
                for qq in range(nq):
                    dst = pl.ds(MAX_DILATION * b * rr + qq * dilation + r, rr, stride=MAX_DILATION)
                    for half in range(ATTN_GROUP_WIDTH // LANES):
                        ls = slice(half * LANES, (half + 1) * LANES)
                        o_ref[half, dst, :] = o[qq * rr:(qq + 1) * rr, ls]
                        l_ref[half, dst, :] = lse[qq * rr:(qq + 1) * rr, ls]

            blocks.append((q, kcat, vcat, has_prev, write))
        _attn_blocks(blocks, pos_of, s_scr, p_scr)


def _dil_attn(qkv, col0, dilation, bsz, seq):
    gw, w = ATTN_GROUP_WIDTH, ATTN_BLOCK
    nt = seq // ATTN_TILE
    n_sc = ATTN_BATCH * ATTN_HEADS_PER_GROUP
    scratch = [pltpu.VMEM((n_sc, w, 2 * w), F32), pltpu.VMEM((n_sc, w, 2 * w), BF16)]
    out_spec = pl.BlockSpec((gw // LANES, ATTN_TILE, LANES), lambda b, i: (0, b * nt + i, 0))
    out_shape = [jax.ShapeDtypeStruct((gw // LANES, bsz * seq, LANES), F32)] * 2
    if dilation == 1:
        bpt = ATTN_TILE // w
        cur = lambda c: pl.BlockSpec((ATTN_TILE, gw), lambda b, i: (b * nt + i, c))
        halo = lambda c: pl.BlockSpec((w, gw), lambda b, i: (jnp.maximum((b * nt + i) * bpt - 1, 0), c))
        body = _dil_attn_nat_body
    else:
        rr = w // (MAX_DILATION // dilation)
        bpt = w // rr
        cur = lambda c: pl.BlockSpec((None, MAX_DILATION, w, gw), lambda b, i: (b, 0, i, c))
        halo = lambda c: pl.BlockSpec((None, MAX_DILATION, rr, gw),
                                      lambda b, i: (b, 0, jnp.maximum(i * bpt - 1, 0), c))
        body = functools.partial(_dil_attn_res_body, dilation=dilation)
    return pl.pallas_call(
        body,
        grid=(bsz, nt),
        in_specs=[cur(col0), cur(col0 + 1), halo(col0 + 1), cur(col0 + 2), halo(col0 + 2)],
        out_specs=[out_spec, out_spec],
        out_shape=out_shape,
        scratch_shapes=scratch,
        compiler_params=_params(2),
        name=f"dil_attn_d{dilation}",
    )(qkv, qkv, qkv, qkv, qkv)


def _mem_kv_body(mem_ref, g_ref, w_ref, k_ref, v_ref):
    n = _rms(mem_ref[...], g_ref[...]).astype(BF16)
    kv = _dot(n, w_ref[...])
    k_ref[...] = kv[:, :MEM_WIDTH].astype(BF16)
    v_ref[...] = kv[:, MEM_WIDTH:].astype(BF16)


def _mem_kv(mem, g, w_kv):
    bsz, mem_len, _ = mem.shape
    blk = pl.BlockSpec((None, mem_len, MEM_WIDTH), lambda b: (b, 0, 0))
    return pl.pallas_call(
        _mem_kv_body,
        grid=(bsz,),
        in_specs=[pl.BlockSpec((None, mem_len, D_MODEL), lambda b: (b, 0, 0)),
                  _resident((1, D_MODEL)), _resident(w_kv.shape)],
        out_specs=[blk, blk],
        out_shape=[jax.ShapeDtypeStruct((bsz, mem_len, MEM_WIDTH), BF16)] * 2,
        compiler_params=_params(1),
        name="mem_kv",
    )(mem, g, w_kv)


def _merge_body(y_ref, o0_ref, o1_ref, o2_ref, l0_ref, l1_ref, l2_ref, mq_ref, mk_ref, mv_ref,
                zg_ref, x_ref, wglut_ref, bglu_ref, wssm_ref, wattn_ref, wmem_ref, wo_ref, bgate_ref,
                h_ref):
    hh = SSM_GROUP_SIZE
    yt = jnp.concatenate([y_ref[:, c * hh:(c + 1) * hh, :].reshape(SSM_WIDTH, SSM_CHUNK)
                          for c in range(y_ref.shape[1] // hh)], axis=1)
    gy = _gelu_tanh(yt)
    glu = _dot(wglut_ref[...], gy.astype(BF16)) + bglu_ref[...]
    y2 = (gy * _sigmoid(glu)).T.astype(BF16)
    br_ssm = _dot(y2, wssm_ref[...])

    halves = []
    for half in range(ATTN_GROUP_WIDTH // LANES):
        l0, l1, l2 = l0_ref[half], l1_ref[half], l2_ref[half]
        mx = jnp.maximum(jnp.maximum(l0, l1), l2)
        e0, e1, e2 = jnp.exp(l0 - mx), jnp.exp(l1 - mx), jnp.exp(l2 - mx)
        halves.append((e0 * o0_ref[half] + e1 * o1_ref[half] + e2 * o2_ref[half]) / (e0 + e1 + e2))
    br_attn = _dot(jnp.concatenate(halves, axis=1).astype(BF16), wattn_ref[...])

    e = MEM_HEAD_DIM
    scale = e ** -0.5
    heads = []
    for h in range(MEM_HEADS):
        sl = slice(h * e, (h + 1) * e)
        s = _dot_nt(mq_ref[:, sl], mk_ref[:, sl]) * scale
        p = jnp.exp(s - jnp.max(s, axis=-1, keepdims=True))
        p = p / jnp.sum(p, axis=-1, keepdims=True)
        heads.append(_dot(p.astype(BF16), mv_ref[:, sl]))
    br_mem = _dot(jnp.concatenate(heads, axis=1).astype(BF16), wmem_ref[...])

    d = D_MODEL
    merged = (_sigmoid(zg_ref[:, 0:d].astype(F32) + bgate_ref[:, 0:d]) * br_ssm
              + _sigmoid(zg_ref[:, d:2 * d].astype(F32) + bgate_ref[:, d:2 * d]) * br_attn
              + _sigmoid(zg_ref[:, 2 * d:3 * d].astype(F32) + bgate_ref[:, 2 * d:3 * d]) * br_mem)
    h_ref[...] = x_ref[...] + _dot(merged.astype(BF16), wo_ref[...])


def _merge(y3, os, ls, mq, mk, mv, zg, x2, w_glu_t, b_glu, w_ssm, w_attn, w_mem, w_o, b_gate,
           bsz, seq, tm):
    nt = seq // tm
    cpt = tm // SSM_CHUNK
    mem_len = mk.shape[1]
    row = lambda b, i: (b * nt + i, 0)
    tok = lambda width: pl.BlockSpec((tm, width), row)
    per_batch = pl.BlockSpec((None, mem_len, MEM_WIDTH), lambda b, i: (b, 0, 0))
    weights = [w_glu_t, b_glu, w_ssm, w_attn, w_mem, w_o, b_gate]
    return pl.pallas_call(
        _merge_body,
        grid=(bsz, nt),
        in_specs=[pl.BlockSpec((SSM_GROUPS, cpt * SSM_GROUP_SIZE, SSM_CHUNK), lambda b, i: (0, b * nt + i, 0))]
        + [pl.BlockSpec((ATTN_GROUP_WIDTH // LANES, tm, LANES), lambda b, i: (0, b * nt + i, 0))] * 6
        + [tok(MEM_WIDTH), per_batch, per_batch, tok(N_BRANCHES * D_MODEL), tok(D_MODEL)]
        + [_resident(w.shape) for w in weights],
        out_specs=tok(D_MODEL),
        out_shape=jax.ShapeDtypeStruct((bsz * seq, D_MODEL), F32),
        compiler_params=_params(2),
        name="merge",
    )(y3, *os, *ls, mq, mk, mv, zg, x2, *weights)


def _mlp_body(h_ref, g2_ref, wup_ref, wdown_ref, gf_ref, out_ref, *, final_norm, ff_chunk):
    h = h_ref[...]
    n2 = _rms(h, g2_ref[...]).astype(BF16)
    acc = h
    for c in range(0, D_FF, ff_chunk):
        a = jnp.maximum(_dot(n2, wup_ref[:, c:c + ff_chunk]), 0.0)
        acc = acc + _dot((a * a).astype(BF16), wdown_ref[c:c + ff_chunk, :])
    out_ref[...] = _rms(acc, gf_ref[...]) if final_norm else acc


def _mlp(h, g2, w_up, w_down, gf, final_norm, tm):
    tokens = h.shape[0]
    row = lambda i: (i, 0)
    return pl.pallas_call(
        functools.partial(_mlp_body, final_norm=final_norm, ff_chunk=D_MODEL),
        grid=(tokens // tm,),
        in_specs=[pl.BlockSpec((tm, D_MODEL), row), _resident((1, D_MODEL)),
                  _resident(w_up.shape), _resident(w_down.shape), _resident((1, D_MODEL))],
        out_specs=pl.BlockSpec((tm, D_MODEL), row),
        out_shape=jax.ShapeDtypeStruct((tokens, D_MODEL), F32),
        compiler_params=_params(1),
        name="mlp",
    )(h, g2, w_up, w_down, gf)


def kernel(x, mem, norm1_g, mem_norm_g, w_in, b_gate, ssm_lambda_re, ssm_lambda_im, ssm_log_dt, ssm_b_re, ssm_b_im, ssm_c_re, ssm_c_im, ssm_d, w_glu, b_glu, w_ssm_br, w_attn_br, w_mem_kv, w_mem_br, w_o, norm2_g, w_up, w_down, final_g):
    bsz, seq, d_model = x.shape
    depth = w_in.shape[0]
    assert d_model == D_MODEL and seq % ATTN_TILE == 0
    assert tuple(w // d for w, d in ATTN_PATTERNS) == (ATTN_BLOCK,) * 3
    assert tuple(d for _, d in ATTN_PATTERNS) == (1, 4, 16)
    tm = 512
    h = x.reshape(bsz * seq, D_MODEL)
    row = lambda v: v.reshape(1, -1).astype(F32)
    aw, gw = ATTN_WIDTH, ATTN_GROUP_WIDTH
    o_q, o_k, o_v = SSM_WIDTH, SSM_WIDTH + aw, SSM_WIDTH + 2 * aw
    o_mq = SSM_WIDTH + 3 * aw
    o_g = o_mq + MEM_WIDTH
    for i in range(depth):
        wi = w_in[i]
        cols = lambda g: [wi[:, o + g * gw:o + (g + 1) * gw] for o in (o_q, o_k, o_v)]
        w_qkv0 = jnp.concatenate(cols(0), axis=1).astype(BF16)
        w_qkvd = jnp.concatenate(cols(1) + cols(2), axis=1).astype(BF16)
        u3, qkv0, qkvd, mq, zg = _in_proj(
            h, row(norm1_g[i]), wi[:, :SSM_WIDTH].T.astype(BF16), w_qkv0, w_qkvd,
            wi[:, o_mq:o_g].astype(BF16), wi[:, o_g:].astype(BF16), bsz, seq, tm)

        ops = _ssm_operators(ssm_lambda_re[i], ssm_lambda_im[i], ssm_log_dt[i], ssm_b_re[i],
                             ssm_b_im[i], ssm_c_re[i], ssm_c_im[i], ssm_d[i])
        y3 = _ssm(u3, *ops, bsz)

        outs = [_dil_attn(qkv0, 0, 1, bsz, seq), _dil_attn(qkvd, 0, 4, bsz, seq),
                _dil_attn(qkvd, 3, 16, bsz, seq)]
        os, ls = [o for o, _ in outs], [l for _, l in outs]

        mk, mv = _mem_kv(mem, row(mem_norm_g[i]), w_mem_kv[i].astype(BF16))

        h = _merge(y3, os, ls, mq, mk, mv, zg, h,
                   w_glu[i].T.astype(BF16), b_glu[i].reshape(-1, 1).astype(F32),
                   w_ssm_br[i].astype(BF16), w_attn_br[i].astype(BF16), w_mem_br[i].astype(BF16),
                   w_o[i].astype(BF16), row(b_gate[i]), bsz, seq, tm)

        h = _mlp(h, row(norm2_g[i]), w_up[i].astype(BF16), w_down[i].astype(BF16), row(final_g),
                 final_norm=(i == depth - 1), tm=tm)
    return h.reshape(bsz, seq, D_MODEL)
```

```python
import functools
import math

import jax
import jax.numpy as jnp
from jax import lax
from jax.experimental import pallas as pl
from jax.experimental.pallas import tpu as pltpu

D_MODEL = 1024
SSM_GROUP_SIZE = 16
SSM_GROUPS = 32
SSM_WIDTH = SSM_GROUP_SIZE * SSM_GROUPS
SSM_STATE = 64
SSM_CHUNK = 128
ATTN_PATTERNS = ((128, 1), (512, 4), (2048, 16))
ATTN_HEADS_PER_GROUP = 4
ATTN_HEAD_DIM = 64
ATTN_GROUP_WIDTH = ATTN_HEADS_PER_GROUP * ATTN_HEAD_DIM
ATTN_WIDTH = ATTN_GROUP_WIDTH * len(ATTN_PATTERNS)
ATTN_BLOCK = 128
MAX_DILATION = 16
ATTN_TILE = ATTN_BLOCK * MAX_DILATION
ATTN_BATCH = 4
MEM_HEADS = 4
MEM_HEAD_DIM = 128
MEM_WIDTH = MEM_HEADS * MEM_HEAD_DIM
N_BRANCHES = 3
D_FF = 4 * D_MODEL
RMS_EPS = 1e-6
NEG_INF = -1e30

LANES = 128

VMEM_LIMIT_BYTES = 56 * 1024 * 1024

F32 = jnp.float32
BF16 = jnp.bfloat16


def _params(n_axes):
    return pltpu.CompilerParams(
        dimension_semantics=("arbitrary",) * n_axes, vmem_limit_bytes=VMEM_LIMIT_BYTES)


def _resident(shape):
    nd = len(shape)
    return pl.BlockSpec(shape, lambda *_: (0,) * nd, pipeline_mode=pl.Buffered(1))


def _rms(x, g):
    ms = jnp.mean(x * x, axis=-1, keepdims=True)
    return x * lax.rsqrt(ms + RMS_EPS) * g


def _sigmoid(x):
    return 0.5 * jnp.tanh(0.5 * x) + 0.5


def _gelu_tanh(x):
    c = math.sqrt(2.0 / math.pi)
    return x * (0.5 * (1.0 + jnp.tanh(c * (x + 0.044715 * (x * x * x)))))


def _dot(a, b):
    return jnp.dot(a, b, preferred_element_type=F32)


def _dot_nt(a, b):
    return lax.dot_general(a, b, (((1,), (1,)), ((), ())), preferred_element_type=F32)


def _in_proj_body(x_ref, g_ref, wut_ref, wqkv0_ref, wqkvd_ref, wmq_ref, wg_ref,
                  u_ref, qkv0_ref, qkvd_ref, mq_ref, zg_ref, n_scr):
    tm = x_ref.shape[0]
    nf = _rms(x_ref[...], g_ref[...])
    n = nf.astype(BF16)
    for c in range(D_MODEL // LANES):
        n_scr[c] = nf[:, c * LANES:(c + 1) * LANES]
    ut = _dot_nt(wut_ref[...], n)
    for c in range(tm // SSM_CHUNK):
        u_ref[:, c * SSM_GROUP_SIZE:(c + 1) * SSM_GROUP_SIZE, :] = (
            ut[:, c * SSM_CHUNK:(c + 1) * SSM_CHUNK].reshape(SSM_GROUPS, SSM_GROUP_SIZE, SSM_CHUNK))
    qkv0_ref[...] = _dot(n, wqkv0_ref[...]).astype(BF16)
    mq_ref[...] = _dot(n, wmq_ref[...]).astype(BF16)
    for c in range(0, N_BRANCHES * D_MODEL, D_MODEL):
        zg_ref[:, c:c + D_MODEL] = _dot(n, wg_ref[:, c:c + D_MODEL]).astype(BF16)
    rows = tm // MAX_DILATION
    n_perm = jnp.concatenate(
        [jnp.concatenate([n_scr[c, pl.ds(r, rows, stride=MAX_DILATION), :]
                          for c in range(D_MODEL // LANES)], axis=1)
         for r in range(MAX_DILATION)], axis=0)
    zp = _dot(n_perm.astype(BF16), wqkvd_ref[...]).astype(BF16)
    for r in range(MAX_DILATION):
        qkvd_ref[r] = zp[r * rows:(r + 1) * rows]


def _in_proj(x2, g, w_ut, w_qkv0, w_qkvd, w_mq, w_g, bsz, seq, tm):
    tokens = bsz * seq
    nt = seq // tm
    cpt = tm // SSM_CHUNK
    n_chunks = seq // SSM_CHUNK
    wd = w_qkvd.shape[1]
    row = lambda b, i: (b * nt + i, 0)
    return pl.pallas_call(
        _in_proj_body,
        grid=(bsz, nt),
        in_specs=[
            pl.BlockSpec((tm, D_MODEL), row),
            _resident((1, D_MODEL)),
            _resident(w_ut.shape), _resident(w_qkv0.shape), _resident(w_qkvd.shape),
            _resident(w_mq.shape), _resident(w_g.shape),
        ],
        out_specs=[
            pl.BlockSpec((SSM_GROUPS, cpt * SSM_GROUP_SIZE, SSM_CHUNK), lambda b, i: (0, b * nt + i, 0)),
            pl.BlockSpec((tm, ATTN_WIDTH), row),
            pl.BlockSpec((None, MAX_DILATION, tm // MAX_DILATION, wd), lambda b, i: (b, 0, i, 0)),
            pl.BlockSpec((tm, MEM_WIDTH), row),
            pl.BlockSpec((tm, N_BRANCHES * D_MODEL), row),
        ],
        out_shape=[
            jax.ShapeDtypeStruct((SSM_GROUPS, bsz * n_chunks * SSM_GROUP_SIZE, SSM_CHUNK), F32),
            jax.ShapeDtypeStruct((tokens, ATTN_WIDTH), BF16),
            jax.ShapeDtypeStruct((bsz, MAX_DILATION, seq // MAX_DILATION, wd), BF16),
            jax.ShapeDtypeStruct((tokens, MEM_WIDTH), BF16),
            jax.ShapeDtypeStruct((tokens, N_BRANCHES * D_MODEL), BF16),
        ],
        scratch_shapes=[pltpu.VMEM((D_MODEL // LANES, tm, LANES), F32)],
        compiler_params=_params(2),
        name="in_proj",
    )(x2, g, w_ut, w_qkv0, w_qkvd, w_mq, w_g)


def _ssm_body(u_ref, pw_ref, bc_ref, dtap_ref, a_ref, y_ref,
              k_scr, t_scr, m_scr, nt_scr, z_scr, sp_scr, *, bsz):
    hh = SSM_GROUP_SIZE
    cc = u_ref.shape[1]
    rows = u_ref.shape[0] // hh
    n_chunks = rows // bsz
    half = 2 * SSM_STATE

    q_re, q_im, r_re, r_im = pw_ref[0], pw_ref[1], pw_ref[2], pw_ref[3]
    b_re, b_im, c_re, c_im = bc_ref[0], bc_ref[1], bc_ref[2], bc_ref[3]
    w_re, w_im = [], []
    for h in range(hh):
        rs = slice(h * cc, (h + 1) * cc)
        bre, bim, cre, cim = b_re[h:h + 1], b_im[h:h + 1], c_re[h:h + 1], c_im[h:h + 1]
        m_scr[rs, 0:half] = (q_re * bre - q_im * bim).astype(BF16)
        m_scr[rs, half:2 * half] = (q_re * bim + q_im * bre).astype(BF16)
        nt_scr[rs, 0:half] = (r_re * cre - r_im * cim).astype(BF16)
        nt_scr[rs, half:2 * half] = (-(r_im * cre + r_re * cim)).astype(BF16)
        w_re.append(c_re * bre - c_im * bim)
        w_im.append(c_re * bim + c_im * bre)
    hp = lax.Precision.HIGHEST
    k_scr[...] = (jnp.dot(jnp.concatenate(w_re, axis=0), pw_ref[4], precision=hp, preferred_element_type=F32)
                  - jnp.dot(jnp.concatenate(w_im, axis=0), pw_ref[5], precision=hp, preferred_element_type=F32)
                  + dtap_ref[...])

    srow = lax.broadcasted_iota(jnp.int32, (cc, cc), 0)
    tcol = lax.broadcasted_iota(jnp.int32, (cc, cc), 1)
    causal = tcol >= srow

    def build(h, carry):
        taps = k_scr[pl.ds(pl.multiple_of(h * hh, hh), hh), :]
        for hp in range(hh):
            kb = jnp.broadcast_to(taps[hp:hp + 1, :], (cc, cc))
            blk = pltpu.roll(kb, 0, 1, stride=1, stride_axis=0)
            t_scr[pl.ds(pl.multiple_of(h * cc, cc), cc), hp * cc:(hp + 1) * cc] = (
                jnp.where(causal, blk, 0.0).astype(BF16))
        return carry

    lax.fori_loop(0, hh, build, 0)

    x = jnp.concatenate([u_ref[pl.ds(h, rows, stride=hh), :].astype(BF16) for h in range(hh)],
                        axis=1)
    z_scr[...] = _dot(x, m_scr[...]).reshape(bsz, n_chunks, 2 * half)
    ar = a_ref[0:1, :].reshape(1, 1, half)
    ai = a_ref[1:2, :].reshape(1, 1, half)
    sr = jnp.zeros((bsz, 1, half), F32)
    si = jnp.zeros((bsz, 1, half), F32)
    for c in range(n_chunks):
        sp_scr[:, c:c + 1, 0:half] = sr
        sp_scr[:, c:c + 1, half:2 * half] = si
        zr = z_scr[:, c:c + 1, 0:half]
        zi = z_scr[:, c:c + 1, half:2 * half]
        sr, si = ar * sr - ai * si + zr, ar * si + ai * sr + zi
    sp = sp_scr[...].reshape(rows, 2 * half).astype(BF16)
    y = _dot(x, t_scr[...]) + _dot_nt(sp, nt_scr[...])
    for h in range(hh):
        y_ref[pl.ds(h, rows, stride=hh), :] = y[:, h * cc:(h + 1) * cc]


def _ssm(u3, pw, bc, dtap, a_pow, bsz):
    hh = SSM_GROUP_SIZE
    _, rows16, cc = u3.shape
    n_chunks = rows16 // hh // bsz
    width = hh * cc
    half = 2 * SSM_STATE
    blk = pl.BlockSpec((None, rows16, cc), lambda g: (g, 0, 0))
    per_group = lambda a: pl.BlockSpec((None,) + a.shape[1:], lambda g: (g,) + (0,) * (a.ndim - 1))
    return pl.pallas_call(
        functools.partial(_ssm_body, bsz=bsz),
        grid=(SSM_GROUPS,),
        in_specs=[blk, per_group(pw), per_group(bc), per_group(dtap), per_group(a_pow)],
        out_specs=blk,
        out_shape=jax.ShapeDtypeStruct(u3.shape, F32),
        scratch_shapes=[
            pltpu.VMEM((hh * hh, cc), F32),
            pltpu.VMEM((width, width), BF16),
            pltpu.VMEM((width, 2 * half), BF16),
            pltpu.VMEM((width, 2 * half), BF16),
            pltpu.VMEM((bsz, n_chunks, 2 * half), F32),
            pltpu.VMEM((bsz, n_chunks, 2 * half), F32),
        ],
        compiler_params=_params(1),
        name="ssm",
    )(u3, pw, bc, dtap, a_pow)


def _ssm_operators(lam_re, lam_im, log_dt, b_re, b_im, c_re, c_im, d_skip):
    cc, pp, hh, gg = SSM_CHUNK, SSM_STATE, SSM_GROUP_SIZE, SSM_GROUPS
    lr, li = lam_re.astype(F32), lam_im.astype(F32)
    dt = jnp.exp(log_dt.astype(F32))[:, None]
    mag = jnp.exp(lr * dt)
    a_re, a_im = mag * jnp.cos(li * dt), mag * jnp.sin(li * dt)
    nr, ni = a_re - 1.0, a_im
    den = lr * lr + li * li
    coef_re = (nr * lr + ni * li) / den
    coef_im = (ni * lr - nr * li) / den
    br_, bi_ = b_re.astype(F32), b_im.astype(F32)
    bb_re = coef_re[..., None] * br_ - coef_im[..., None] * bi_
    bb_im = coef_re[..., None] * bi_ + coef_im[..., None] * br_
    cr, ci = c_re.astype(F32), c_im.astype(F32)
    dd = d_skip.astype(F32)

    k = jnp.arange(cc + 1, dtype=F32)[:, None, None]
    pmag = jnp.exp(k * (lr * dt)[None])
    pang = k * (li * dt)[None]
    p_re, p_im = pmag * jnp.cos(pang), pmag * jnp.sin(pang)

    lane_pad = lambda v: jnp.pad(v, [(0, 0)] * (v.ndim - 1) + [(0, pp)])
    by_group = lambda v: lane_pad(v.transpose(1, 0, 2))
    flip = cc - 1 - jnp.arange(cc)
    pt = lambda v: jnp.pad(v[:cc].transpose(1, 2, 0), ((0, 0), (0, pp), (0, 0)))
    pw = jnp.stack([by_group(p_re[flip]), by_group(p_im[flip]),
                    by_group(p_re[1:cc + 1]), by_group(p_im[1:cc + 1]),
                    pt(p_re), pt(p_im)], axis=1)
    bc = jnp.stack([lane_pad(bb_re.transpose(0, 2, 1)), lane_pad(bb_im.transpose(0, 2, 1)),
                    lane_pad(cr), lane_pad(ci)], axis=1)
    diag = jnp.arange(hh) * (hh + 1)
    dtap = jnp.zeros((gg, hh * hh, cc), F32).at[:, diag, 0].set(dd)
    a_pow = jnp.zeros((gg, 8, 2 * pp), F32)
    a_pow = a_pow.at[:, 0].set(lane_pad(p_re[cc])).at[:, 1].set(lane_pad(p_im[cc]))
    return pw, bc, dtap, a_pow


def _attn_tile(batches, pos_of, s_scr, p_scr):
    w, e, nh = ATTN_BLOCK, ATTN_HEAD_DIM, ATTN_HEADS_PER_GROUP
    lane = lax.broadcasted_iota(jnp.int32, (w, 2 * w), 1)
    pkey = pos_of(lane & (w - 1))
    pqry = pos_of(lax.broadcasted_iota(jnp.int32, (w, 2 * w), 0))
    mask = jnp.logical_or(jnp.logical_and(lane < w, pkey >= pqry),
                          jnp.logical_and(lane >= w, pkey <= pqry))
    head_of_lane = lane // e
    scale = e ** -0.5
    slot0 = [0]
    for blocks in batches:
        slot0.append(slot0[-1] + len(blocks) * nh)

    def scores(k):
        for bi, (q, kcat, _, has_prev, _) in enumerate(batches[k]):
            qs = q * scale
            qm = jnp.concatenate([jnp.where(head_of_lane == h, qs, 0) for h in range(nh)], axis=0)
            s = _dot_nt(qm, kcat).reshape(nh, w, 2 * w)
            if has_prev is not None:
                s = jnp.where(jnp.logical_or(lane >= w, has_prev)[None], s, NEG_INF)
            s_scr[slot0[k] + bi * nh:slot0[k] + (bi + 1) * nh] = s

    def softmax(k):
        s_all = jnp.where(mask[None], s_scr[slot0[k]:slot0[k + 1]], NEG_INF)
        mx = jnp.max(s_all, axis=-1, keepdims=True)
        p = jnp.exp(s_all - mx)
        den = jnp.sum(p, axis=-1, keepdims=True)
        p_scr[slot0[k]:slot0[k + 1]] = p.astype(BF16)
        return mx + jnp.log(den), 1.0 / den

    def values(k, lse, inv):
        for bi, (_, _, vcat, _, write) in enumerate(batches[k]):
            lo = slot0[k] + bi * nh
            pv = _dot(p_scr[lo:lo + nh].reshape(nh * w, 2 * w), vcat).reshape(nh, w, nh * e)
            outs = [pv[h][:, h * e:(h + 1) * e] * inv[bi * nh + h] for h in range(nh)]
            lses = [jnp.broadcast_to(lse[bi * nh + h], (w, e)) for h in range(nh)]
            write(jnp.concatenate(outs, axis=1), jnp.concatenate(lses, axis=1))

    scores(0)
    for k in range(len(batches)):
        if k + 1 < len(batches):
            scores(k + 1)
        values(k, *softmax(k))


def _dil_attn_nat_body(q_ref, kc_ref, kh_ref, vc_ref, vh_ref, o_ref, l_ref, s_scr, p_scr):
    i = pl.program_id(1)
    w = ATTN_BLOCK
    pos_of = lambda idx: idx
    n_blk = q_ref.shape[0] // w
    batches = []
    for j0 in range(0, n_blk, ATTN_BATCH):
        blocks = []
        for j in range(j0, j0 + ATTN_BATCH):
            rs = slice(j * w, (j + 1) * w)
            if j == 0:
                kcat = jnp.concatenate([kh_ref[...], kc_ref[rs, :]], axis=0)
                vcat = jnp.concatenate([vh_ref[...], vc_ref[rs, :]], axis=0)
                has_prev = i > 0
            else:
                kcat, vcat, has_prev = kc_ref[(j - 1) * w:(j + 1) * w, :], vc_ref[(j - 1) * w:(j + 1) * w, :], None

            def write(o, lse, rs=rs):
                for half in range(ATTN_GROUP_WIDTH // LANES):
                    ls = slice(half * LANES, (half + 1) * LANES)
                    o_ref[half, rs, :] = o[:, ls]
                    l_ref[half, rs, :] = lse[:, ls]

            blocks.append((q_ref[rs, :], kcat, vcat, has_prev, write))
        batches.append(blocks)
    _attn_tile(batches, pos_of, s_scr, p_scr)


def _dil_attn_res_body(q_ref, kc_ref, kh_ref, vc_ref, vh_ref, o_ref, l_ref, s_scr, p_scr, *, dilation):
    i = pl.program_id(1)
    w = ATTN_BLOCK
    nq = MAX_DILATION // dilation
    rr = w // nq
    shift = rr.bit_length() - 1
    pos_of = lambda idx: nq * (idx & (rr - 1)) + (idx >> shift)

    def slabs(ref, r, lo):
        return [ref[q * dilation + r, lo:lo + rr, :] for q in range(nq)]

    todo = [(r, b) for r in range(dilation) for b in range(nq)]
    batches = []
    for j0 in range(0, len(todo), ATTN_BATCH):
        blocks = []
        for r, b in todo[j0:j0 + ATTN_BATCH]:
            if b == 0:
                kprev = [kh_ref[q * dilation + r] for q in range(nq)]
                vprev = [vh_ref[q * dilation + r] for q in range(nq)]
                has_prev = i > 0
            else:
                kprev, vprev, has_prev = slabs(kc_ref, r, (b - 1) * rr), slabs(vc_ref, r, (b - 1) * rr), None
            kcat = jnp.concatenate(kprev + slabs(kc_ref, r, b * rr), axis=0)
            vcat = jnp.concatenate(vprev + slabs(vc_ref, r, b * rr), axis=0)
            q = jnp.concatenate(slabs(q_ref, r, b * rr), axis=0)

            def write(o, lse, r=r, b=b):
                for qq in range(nq):
                    dst = pl.ds(MAX_DILATION * b * rr + qq * dilation + r, rr, stride=MAX_DILATION)
                    for half in range(ATTN_GROUP_WIDTH // LANES):
                        ls = slice(half * LANES, (half + 1) * LANES)
                        o_ref[half, dst, :] = o[qq * rr:(qq + 1) * rr, ls]
                        l_ref[half, dst, :] = lse[qq * rr:(qq + 1) * rr, ls]

            blocks.append((q, kcat, vcat, has_prev, write))
        batches.append(blocks)
    _attn_tile(batches, pos_of, s_scr, p_scr)


def _dil_attn(qkv, col0, dilation, bsz, seq):
    gw, w = ATTN_GROUP_WIDTH, ATTN_BLOCK
    nt = seq // ATTN_TILE
    n_sc = ATTN_TILE // w * ATTN_HEADS_PER_GROUP
    scratch = [pltpu.VMEM((n_sc, w, 2 * w), F32), pltpu.VMEM((n_sc, w, 2 * w), BF16)]
    out_spec = pl.BlockSpec((gw // LANES, ATTN_TILE, LANES), lambda b, i: (0, b * nt + i, 0))
    out_shape = [jax.ShapeDtypeStruct((gw // LANES, bsz * seq, LANES), F32)] * 2
    if dilation == 1:
        bpt = ATTN_TILE // w
        cur = lambda c: pl.BlockSpec((ATTN_TILE, gw), lambda b, i: (b * nt + i, c))
        halo = lambda c: pl.BlockSpec((w, gw), lambda b, i: (jnp.maximum((b * nt + i) * bpt - 1, 0), c))
        body = _dil_attn_nat_body
    else:
        rr = w // (MAX_DILATION // dilation)
        bpt = w // rr
        cur = lambda c: pl.BlockSpec((None, MAX_DILATION, w, gw), lambda b, i: (b, 0, i, c))
        halo = lambda c: pl.BlockSpec((None, MAX_DILATION, rr, gw),
                                      lambda b, i: (b, 0, jnp.maximum(i * bpt - 1, 0), c))
        body = functools.partial(_dil_attn_res_body, dilation=dilation)
    return pl.pallas_call(
        body,
        grid=(bsz, nt),
        in_specs=[cur(col0), cur(col0 + 1), halo(col0 + 1), cur(col0 + 2), halo(col0 + 2)],
        out_specs=[out_spec, out_spec],
        out_shape=out_shape,
        scratch_shapes=scratch,
        compiler_params=_params(2),
        name=f"dil_attn_d{dilation}",
    )(qkv, qkv, qkv, qkv, qkv)


def _mem_kv_body(mem_ref, g_ref, w_ref, k_ref, v_ref):
    n = _rms(mem_ref[...], g_ref[...]).astype(BF16)
    kv = _dot(n, w_ref[...])
    k_ref[...] = kv[:, :MEM_WIDTH].astype(BF16)
    v_ref[...] = kv[:, MEM_WIDTH:].astype(BF16)


def _mem_kv(mem, g, w_kv):
    bsz, mem_len, _ = mem.shape
    blk = pl.BlockSpec((None, mem_len, MEM_WIDTH), lambda b: (b, 0, 0))
    return pl.pallas_call(
        _mem_kv_body,
        grid=(bsz,),
        in_specs=[pl.BlockSpec((None, mem_len, D_MODEL), lambda b: (b, 0, 0)),
                  _resident((1, D_MODEL)), _resident(w_kv.shape)],
        out_specs=[blk, blk],
        out_shape=[jax.ShapeDtypeStruct((bsz, mem_len, MEM_WIDTH), BF16)] * 2,
        compiler_params=_params(1),
        name="mem_kv",
    )(mem, g, w_kv)


def _merge_body(y_ref, o0_ref, o1_ref, o2_ref, l0_ref, l1_ref, l2_ref, mq_ref, mk_ref, mv_ref,
                zg_ref, x_ref, wglut_ref, bglu_ref, wssm_ref, wattn_ref, wmem_ref, wo_ref, bgate_ref,
                h_ref):
    hh = SSM_GROUP_SIZE
    yt = jnp.concatenate([y_ref[:, c * hh:(c + 1) * hh, :].reshape(SSM_WIDTH, SSM_CHUNK)
                          for c in range(y_ref.shape[1] // hh)], axis=1)
    gy = _gelu_tanh(yt)
    glu = _dot(wglut_ref[...], gy.astype(BF16)) + bglu_ref[...]
    y2 = (gy * _sigmoid(glu)).T.astype(BF16)
    br_ssm = _dot(y2, wssm_ref[...])

    halves = []
    for half in range(ATTN_GROUP_WIDTH // LANES):
        l0, l1, l2 = l0_ref[half], l1_ref[half], l2_ref[half]
        mx = jnp.maximum(jnp.maximum(l0, l1), l2)
        e0, e1, e2 = jnp.exp(l0 - mx), jnp.exp(l1 - mx), jnp.exp(l2 - mx)
        halves.append((e0 * o0_ref[half] + e1 * o1_ref[half] + e2 * o2_ref[half]) / (e0 + e1 + e2))
    br_attn = _dot(jnp.concatenate(halves, axis=1).astype(BF16), wattn_ref[...])

    e = MEM_HEAD_DIM
    scale = e ** -0.5
    heads = []
    for h in range(MEM_HEADS):
        sl = slice(h * e, (h + 1) * e)
        s = _dot_nt(mq_ref[:, sl], mk_ref[:, sl]) * scale
        p = jnp.exp(s - jnp.max(s, axis=-1, keepdims=True))
        inv = 1.0 / jnp.sum(p, axis=-1, keepdims=True)
        heads.append(_dot(p.astype(BF16), mv_ref[:, sl]) * inv)
    br_mem = _dot(jnp.concatenate(heads, axis=1).astype(BF16), wmem_ref[...])

    d = D_MODEL
    merged = (_sigmoid(zg_ref[:, 0:d].astype(F32) + bgate_ref[:, 0:d]) * br_ssm
              + _sigmoid(zg_ref[:, d:2 * d].astype(F32) + bgate_ref[:, d:2 * d]) * br_attn
              + _sigmoid(zg_ref[:, 2 * d:3 * d].astype(F32) + bgate_ref[:, 2 * d:3 * d]) * br_mem)
    h_ref[...] = x_ref[...] + _dot(merged.astype(BF16), wo_ref[...])


def _merge(y3, os, ls, mq, mk, mv, zg, x2, w_glu_t, b_glu, w_ssm, w_attn, w_mem, w_o, b_gate,
           bsz, seq, tm):
    nt = seq // tm
    cpt = tm // SSM_CHUNK
    mem_len = mk.shape[1]
    row = lambda b, i: (b * nt + i, 0)
    tok = lambda width: pl.BlockSpec((tm, width), row)
    per_batch = pl.BlockSpec((None, mem_len, MEM_WIDTH), lambda b, i: (b, 0, 0))
    weights = [w_glu_t, b_glu, w_ssm, w_attn, w_mem, w_o, b_gate]
    return pl.pallas_call(
        _merge_body,
        grid=(bsz, nt),
        in_specs=[pl.BlockSpec((SSM_GROUPS, cpt * SSM_GROUP_SIZE, SSM_CHUNK), lambda b, i: (0, b * nt + i, 0))]
        + [pl.BlockSpec((ATTN_GROUP_WIDTH // LANES, tm, LANES), lambda b, i: (0, b * nt + i, 0))] * 6
        + [tok(MEM_WIDTH), per_batch, per_batch, tok(N_BRANCHES * D_MODEL), tok(D_MODEL)]
        + [_resident(w.shape) for w in weights],
        out_specs=tok(D_MODEL),
        out_shape=jax.ShapeDtypeStruct((bsz * seq, D_MODEL), F32),
        compiler_params=_params(2),
        name="merge",
    )(y3, *os, *ls, mq, mk, mv, zg, x2, *weights)


def _mlp_body(h_ref, g2_ref, wup_ref, wdown_ref, gf_ref, out_ref, *, final_norm, ff_chunk):
    h = h_ref[...]
    n2 = _rms(h, g2_ref[...]).astype(BF16)
    acc = h
    for c in range(0, D_FF, ff_chunk):
        a = jnp.maximum(_dot(n2, wup_ref[:, c:c + ff_chunk]), 0.0)
        acc = acc + _dot((a * a).astype(BF16), wdown_ref[c:c + ff_chunk, :])
    out_ref[...] = _rms(acc, gf_ref[...]) if final_norm else acc


def _mlp(h, g2, w_up, w_down, gf, final_norm, tm):
    tokens = h.shape[0]
    row = lambda i: (i, 0)
    return pl.pallas_call(
        functools.partial(_mlp_body, final_norm=final_norm, ff_chunk=D_MODEL),
        grid=(tokens // tm,),
        in_specs=[pl.BlockSpec((tm, D_MODEL), row), _resident((1, D_MODEL)),
                  _resident(w_up.shape), _resident(w_down.shape), _resident((1, D_MODEL))],
        out_specs=pl.BlockSpec((tm, D_MODEL), row),
        out_shape=jax.ShapeDtypeStruct((tokens, D_MODEL), F32),
        compiler_params=_params(1),
        name="mlp",
    )(h, g2, w_up, w_down, gf)


def kernel(x, mem, norm1_g, mem_norm_g, w_in, b_gate, ssm_lambda_re, ssm_lambda_im, ssm_log_dt, ssm_b_re, ssm_b_im, ssm_c_re, ssm_c_im, ssm_d, w_glu, b_glu, w_ssm_br, w_attn_br, w_mem_kv, w_mem_br, w_o, norm2_g, w_up, w_down, final_g):
    bsz, seq, d_model = x.shape
    depth = w_in.shape[0]
    assert d_model == D_MODEL and seq % ATTN_TILE == 0
    assert tuple(w // d for w, d in ATTN_PATTERNS) == (ATTN_BLOCK,) * 3
    assert tuple(d for _, d in ATTN_PATTERNS) == (1, 4, 16)
    tm = 512
    h = x.reshape(bsz * seq, D_MODEL)
    row = lambda v: v.reshape(1, -1).astype(F32)
    aw, gw = ATTN_WIDTH, ATTN_GROUP_WIDTH
    o_q, o_k, o_v = SSM_WIDTH, SSM_WIDTH + aw, SSM_WIDTH + 2 * aw
    o_mq = SSM_WIDTH + 3 * aw
    o_g = o_mq + MEM_WIDTH
    for i in range(depth):
        wi = w_in[i]
        cols = lambda g: [wi[:, o + g * gw:o + (g + 1) * gw] for o in (o_q, o_k, o_v)]
        w_qkv0 = jnp.concatenate(cols(0), axis=1).astype(BF16)
        w_qkvd = jnp.concatenate(cols(1) + cols(2), axis=1).astype(BF16)
        u3, qkv0, qkvd, mq, zg = _in_proj(
            h, row(norm1_g[i]), wi[:, :SSM_WIDTH].T.astype(BF16), w_qkv0, w_qkvd,
            wi[:, o_mq:o_g].astype(BF16), wi[:, o_g:].astype(BF16), bsz, seq, tm)

        ops = _ssm_operators(ssm_lambda_re[i], ssm_lambda_im[i], ssm_log_dt[i], ssm_b_re[i],
                             ssm_b_im[i], ssm_c_re[i], ssm_c_im[i], ssm_d[i])
        y3 = _ssm(u3, *ops, bsz)

        outs = [_dil_attn(qkv0, 0, 1, bsz, seq), _dil_attn(qkvd, 0, 4, bsz, seq),
                _dil_attn(qkvd, 3, 16, bsz, seq)]
        os, ls = [o for o, _ in outs], [l for _, l in outs]

        mk, mv = _mem_kv(mem, row(mem_norm_g[i]), w_mem_kv[i].astype(BF16))

        h = _merge(y3, os, ls, mq, mk, mv, zg, h,
                   w_glu[i].T.astype(BF16), b_glu[i].reshape(-1, 1).astype(F32),
                   w_ssm_br[i].astype(BF16), w_attn_br[i].astype(BF16), w_mem_br[i].astype(BF16),
                   w_o[i].astype(BF16), row(b_gate[i]), bsz, seq, tm)

        h = _mlp(h, row(norm2_g[i]), w_up[i].astype(BF16), w_down[i].astype(BF16), row(final_g),
                 final_norm=(i == depth - 1), tm=tm)
    return h.reshape(bsz, seq, D_MODEL)
```

```python
import functools
import math

import jax
import jax.numpy as jnp
from jax import lax
from jax.experimental import pallas as pl
from jax.experimental.pallas import tpu as pltpu

D_MODEL = 1024
SSM_GROUP_SIZE = 16
SSM_GROUPS = 32
SSM_WIDTH = SSM_GROUP_SIZE * SSM_GROUPS
SSM_STATE = 64
SSM_CHUNK = 128
SSM_PANEL = 2
ATTN_PATTERNS = ((128, 1), (512, 4), (2048, 16))
ATTN_HEADS_PER_GROUP = 4
ATTN_HEAD_DIM = 64
ATTN_GROUP_WIDTH = ATTN_HEADS_PER_GROUP * ATTN_HEAD_DIM
ATTN_WIDTH = ATTN_GROUP_WIDTH * len(ATTN_PATTERNS)
ATTN_BLOCK = 128
MAX_DILATION = 16
ATTN_TILE = ATTN_BLOCK * MAX_DILATION
ATTN_BATCH = 4
MEM_HEADS = 4
MEM_HEAD_DIM = 128
MEM_WIDTH = MEM_HEADS * MEM_HEAD_DIM
N_BRANCHES = 3
D_FF = 4 * D_MODEL
RMS_EPS = 1e-6
NEG_INF = -1e30

LANES = 128

VMEM_LIMIT_BYTES = 56 * 1024 * 1024

F32 = jnp.float32
BF16 = jnp.bfloat16


def _params(n_axes):
    return pltpu.CompilerParams(
        dimension_semantics=("arbitrary",) * n_axes, vmem_limit_bytes=VMEM_LIMIT_BYTES)


def _resident(shape):
    nd = len(shape)
    return pl.BlockSpec(shape, lambda *_: (0,) * nd, pipeline_mode=pl.Buffered(1))


def _rms(x, g):
    ms = jnp.mean(x * x, axis=-1, keepdims=True)
    return x * lax.rsqrt(ms + RMS_EPS) * g


def _twice_sigmoid_of_twice(x):
    return 1.0 + jnp.tanh(x)


def _twice_gelu_tanh(x):
    c = math.sqrt(2.0 / math.pi)
    return x * (1.0 + jnp.tanh(x * (c + (c * 0.044715) * (x * x))))


def _dot(a, b):
    return jnp.dot(a, b, preferred_element_type=F32)


def _dot_nt(a, b):
    return lax.dot_general(a, b, (((1,), (1,)), ((), ())), preferred_element_type=F32)


def _in_proj_body(x_ref, g_ref, wut_ref, wqkv0_ref, wqkvd_ref, wmq_ref, wg_ref,
                  u_ref, qkv0_ref, qkvd_ref, mq_ref, zg_ref, n_scr):
    tm = x_ref.shape[0]
    nf = _rms(x_ref[...], g_ref[...])
    n = nf.astype(BF16)
    for c in range(D_MODEL // LANES):
        n_scr[c] = nf[:, c * LANES:(c + 1) * LANES]
    ut = _dot_nt(wut_ref[...], n)
    for c in range(tm // SSM_CHUNK):
        u_ref[:, c * SSM_GROUP_SIZE:(c + 1) * SSM_GROUP_SIZE, :] = (
            ut[:, c * SSM_CHUNK:(c + 1) * SSM_CHUNK].reshape(SSM_GROUPS, SSM_GROUP_SIZE, SSM_CHUNK))
    qkv0_ref[...] = _dot(n, wqkv0_ref[...]).astype(BF16)
    mq_ref[...] = _dot(n, wmq_ref[...]).astype(BF16)
    for c in range(0, N_BRANCHES * D_MODEL, D_MODEL):
        zg_ref[:, c:c + D_MODEL] = _dot(n, wg_ref[:, c:c + D_MODEL]).astype(BF16)
    rows = tm // MAX_DILATION
    n_perm = jnp.concatenate(
        [jnp.concatenate([n_scr[c, pl.ds(r, rows, stride=MAX_DILATION), :]
                          for c in range(D_MODEL // LANES)], axis=1)
         for r in range(MAX_DILATION)], axis=0)
    zp = _dot(n_perm.astype(BF16), wqkvd_ref[...]).astype(BF16)
    for r in range(MAX_DILATION):
        qkvd_ref[r] = zp[r * rows:(r + 1) * rows]


def _in_proj(x2, g, w_ut, w_qkv0, w_qkvd, w_mq, w_g, bsz, seq, tm):
    tokens = bsz * seq
    nt = seq // tm
    cpt = tm // SSM_CHUNK
    n_chunks = seq // SSM_CHUNK
    wd = w_qkvd.shape[1]
    row = lambda b, i: (b * nt + i, 0)
    return pl.pallas_call(
        _in_proj_body,
        grid=(bsz, nt),
        in_specs=[
            pl.BlockSpec((tm, D_MODEL), row),
            _resident((1, D_MODEL)),
            _resident(w_ut.shape), _resident(w_qkv0.shape), _resident(w_qkvd.shape),
            _resident(w_mq.shape), _resident(w_g.shape),
        ],
        out_specs=[
            pl.BlockSpec((SSM_GROUPS, cpt * SSM_GROUP_SIZE, SSM_CHUNK), lambda b, i: (0, b * nt + i, 0)),
            pl.BlockSpec((tm, ATTN_WIDTH), row),
            pl.BlockSpec((None, MAX_DILATION, tm // MAX_DILATION, wd), lambda b, i: (b, 0, i, 0)),
            pl.BlockSpec((tm, MEM_WIDTH), row),
            pl.BlockSpec((tm, N_BRANCHES * D_MODEL), row),
        ],
        out_shape=[
            jax.ShapeDtypeStruct((SSM_GROUPS, bsz * n_chunks * SSM_GROUP_SIZE, SSM_CHUNK), F32),
            jax.ShapeDtypeStruct((tokens, ATTN_WIDTH), BF16),
            jax.ShapeDtypeStruct((bsz, MAX_DILATION, seq // MAX_DILATION, wd), BF16),
            jax.ShapeDtypeStruct((tokens, MEM_WIDTH), BF16),
            jax.ShapeDtypeStruct((tokens, N_BRANCHES * D_MODEL), BF16),
        ],
        scratch_shapes=[pltpu.VMEM((D_MODEL // LANES, tm, LANES), F32)],
        compiler_params=_params(2),
        name="in_proj",
    )(x2, g, w_ut, w_qkv0, w_qkvd, w_mq, w_g)


def _ssm_body(u_ref, pw_ref, bc_ref, dtap_ref, a_ref, y_ref,
              k_scr, t_scr, m_scr, nt_scr, z_scr, sp_scr, *, bsz):
    hh = SSM_GROUP_SIZE
    cc = u_ref.shape[1]
    rows = u_ref.shape[0] // hh
    n_chunks = rows // bsz
    half = 2 * SSM_STATE

    q_re, q_im, r_re, r_im = pw_ref[0], pw_ref[1], pw_ref[2], pw_ref[3]
    b_re, b_im, c_re, c_im = bc_ref[0], bc_ref[1], bc_ref[2], bc_ref[3]
    w_re, w_im = [], []
    for h in range(hh):
        rs = slice(h * cc, (h + 1) * cc)
        bre, bim, cre, cim = b_re[h:h + 1], b_im[h:h + 1], c_re[h:h + 1], c_im[h:h + 1]
        m_scr[rs, 0:half] = (q_re * bre - q_im * bim).astype(BF16)
        m_scr[rs, half:2 * half] = (q_re * bim + q_im * bre).astype(BF16)
        nt_scr[rs, 0:half] = (r_re * cre - r_im * cim).astype(BF16)
        nt_scr[rs, half:2 * half] = (-(r_im * cre + r_re * cim)).astype(BF16)
        w_re.append(c_re * bre - c_im * bim)
        w_im.append(c_re * bim + c_im * bre)
    hp = lax.Precision.HIGHEST
    k_scr[...] = (jnp.dot(jnp.concatenate(w_re, axis=0), pw_ref[4], precision=hp, preferred_element_type=F32)
                  - jnp.dot(jnp.concatenate(w_im, axis=0), pw_ref[5], precision=hp, preferred_element_type=F32)
                  + dtap_ref[...])

    srow = lax.broadcasted_iota(jnp.int32, (cc, cc), 0)
    tcol = lax.broadcasted_iota(jnp.int32, (cc, cc), 1)
    causal = tcol >= srow

    x = jnp.concatenate([u_ref[pl.ds(h, rows, stride=hh), :].astype(BF16) for h in range(hh)],
                        axis=1)
    z_scr[...] = _dot(x, m_scr[...]).reshape(bsz, n_chunks, 2 * half)
    ar = a_ref[0:1, :].reshape(1, 1, half)
    ai = a_ref[1:2, :].reshape(1, 1, half)
    sr = jnp.zeros((bsz, 1, half), F32)
    si = jnp.zeros((bsz, 1, half), F32)
    for c in range(n_chunks):
        sp_scr[:, c:c + 1, 0:half] = sr
        sp_scr[:, c:c + 1, half:2 * half] = si
        zr = z_scr[:, c:c + 1, 0:half]
        zi = z_scr[:, c:c + 1, half:2 * half]
        sr, si = ar * sr - ai * si + zr, ar * si + ai * sr + zi
    sp = sp_scr[...].reshape(rows, 2 * half).astype(BF16)

    for p0 in range(0, hh, SSM_PANEL):
        cols = slice(p0 * cc, (p0 + SSM_PANEL) * cc)
        for h in range(hh):
            for hp in range(p0, p0 + SSM_PANEL):
                kb = jnp.broadcast_to(k_scr[h * hh + hp:h * hh + hp + 1, :], (cc, cc))
                blk = pltpu.roll(kb, 0, 1, stride=1, stride_axis=0)
                t_scr[h * cc:(h + 1) * cc, hp * cc:(hp + 1) * cc] = (
                    jnp.where(causal, blk, 0.0).astype(BF16))
        y = _dot(x, t_scr[:, cols]) + _dot_nt(sp, nt_scr[cols, :])
        for hp in range(p0, p0 + SSM_PANEL):
            y_ref[pl.ds(hp, rows, stride=hh), :] = y[:, (hp - p0) * cc:(hp - p0 + 1) * cc]


def _ssm(u3, pw, bc, dtap, a_pow, bsz):
    hh = SSM_GROUP_SIZE
    _, rows16, cc = u3.shape
    n_chunks = rows16 // hh // bsz
    width = hh * cc
    half = 2 * SSM_STATE
    blk = pl.BlockSpec((None, rows16, cc), lambda g: (g, 0, 0))
    per_group = lambda a: pl.BlockSpec((None,) + a.shape[1:], lambda g: (g,) + (0,) * (a.ndim - 1))
    return pl.pallas_call(
        functools.partial(_ssm_body, bsz=bsz),
        grid=(SSM_GROUPS,),
        in_specs=[blk, per_group(pw), per_group(bc), per_group(dtap), per_group(a_pow)],
        out_specs=blk,
        out_shape=jax.ShapeDtypeStruct(u3.shape, F32),
        scratch_shapes=[
            pltpu.VMEM((hh * hh, cc), F32),
            pltpu.VMEM((width, width), BF16),
            pltpu.VMEM((width, 2 * half), BF16),
            pltpu.VMEM((width, 2 * half), BF16),
            pltpu.VMEM((bsz, n_chunks, 2 * half), F32),
            pltpu.VMEM((bsz, n_chunks, 2 * half), F32),
        ],
        compiler_params=_params(1),
        name="ssm",
    )(u3, pw, bc, dtap, a_pow)


def _ssm_operators(lam_re, lam_im, log_dt, b_re, b_im, c_re, c_im, d_skip):
    cc, pp, hh, gg = SSM_CHUNK, SSM_STATE, SSM_GROUP_SIZE, SSM_GROUPS
    lr, li = lam_re.astype(F32), lam_im.astype(F32)
    dt = jnp.exp(log_dt.astype(F32))[:, None]
    mag = jnp.exp(lr * dt)
    a_re, a_im = mag * jnp.cos(li * dt), mag * jnp.sin(li * dt)
    nr, ni = a_re - 1.0, a_im
    den = lr * lr + li * li
    coef_re = (nr * lr + ni * li) / den
    coef_im = (ni * lr - nr * li) / den
    br_, bi_ = b_re.astype(F32), b_im.astype(F32)
    bb_re = coef_re[..., None] * br_ - coef_im[..., None] * bi_
    bb_im = coef_re[..., None] * bi_ + coef_im[..., None] * br_
    cr, ci = c_re.astype(F32), c_im.astype(F32)
    dd = d_skip.astype(F32)

    k = jnp.arange(cc + 1, dtype=F32)[:, None, None]
    pmag = jnp.exp(k * (lr * dt)[None])
    pang = k * (li * dt)[None]
    p_re, p_im = pmag * jnp.cos(pang), pmag * jnp.sin(pang)

    lane_pad = lambda v: jnp.pad(v, [(0, 0)] * (v.ndim - 1) + [(0, pp)])
    by_group = lambda v: lane_pad(v.transpose(1, 0, 2))
    flip = cc - 1 - jnp.arange(cc)
    pt = lambda v: jnp.pad(v[:cc].transpose(1, 2, 0), ((0, 0), (0, pp), (0, 0)))
    pw = jnp.stack([by_group(p_re[flip]), by_group(p_im[flip]),
                    by_group(p_re[1:cc + 1]), by_group(p_im[1:cc + 1]),
                    pt(p_re), pt(p_im)], axis=1)
    bc = jnp.stack([lane_pad(bb_re.transpose(0, 2, 1)), lane_pad(bb_im.transpose(0, 2, 1)),
                    lane_pad(cr), lane_pad(ci)], axis=1)
    diag = jnp.arange(hh) * (hh + 1)
    dtap = jnp.zeros((gg, hh * hh, cc), F32).at[:, diag, 0].set(dd)
    a_pow = jnp.zeros((gg, 8, 2 * pp), F32)
    a_pow = a_pow.at[:, 0].set(lane_pad(p_re[cc])).at[:, 1].set(lane_pad(p_im[cc]))
    return pw, bc, dtap, a_pow


def _attn_tile(batches, pos_of, s_scr, p_scr):
    w, e, nh = ATTN_BLOCK, ATTN_HEAD_DIM, ATTN_HEADS_PER_GROUP
    lane = lax.broadcasted_iota(jnp.int32, (w, 2 * w), 1)
    pkey = pos_of(lane & (w - 1))
    pqry = pos_of(lax.broadcasted_iota(jnp.int32, (w, 2 * w), 0))
    mask = jnp.logical_or(jnp.logical_and(lane < w, pkey >= pqry),
                          jnp.logical_and(lane >= w, pkey <= pqry))
    head_of_lane = lane // e
    scale = e ** -0.5
    slot0 = [0]
    for blocks in batches:
        slot0.append(slot0[-1] + len(blocks) * nh)

    def scores(k):
        for bi, (q, kcat, _, has_prev, _) in enumerate(batches[k]):
            qs = q * scale
            qm = jnp.concatenate([jnp.where(head_of_lane == h, qs, 0) for h in range(nh)], axis=0)
            s = _dot_nt(qm, kcat).reshape(nh, w, 2 * w)
            if has_prev is not None:
                s = jnp.where(jnp.logical_or(lane >= w, has_prev)[None], s, NEG_INF)
            s_scr[slot0[k] + bi * nh:slot0[k] + (bi + 1) * nh] = s

    def softmax(k):
        s_all = jnp.where(mask[None], s_scr[slot0[k]:slot0[k + 1]], NEG_INF)
        mx = jnp.max(s_all, axis=-1, keepdims=True)
        p = jnp.exp(s_all - mx)
        den = jnp.sum(p, axis=-1, keepdims=True)
        p_scr[slot0[k]:slot0[k + 1]] = p.astype(BF16)
        return mx + jnp.log(den), 1.0 / den

    def values(k, lse, inv):
        for bi, (_, _, vcat, _, write) in enumerate(batches[k]):
            lo = slot0[k] + bi * nh
            pv = _dot(p_scr[lo:lo + nh].reshape(nh * w, 2 * w), vcat).reshape(nh, w, nh * e)
            outs = [pv[h][:, h * e:(h + 1) * e] * inv[bi * nh + h] for h in range(nh)]
            lses = [jnp.broadcast_to(lse[bi * nh + h], (w, e)) for h in range(nh)]
            write(jnp.concatenate(outs, axis=1), jnp.concatenate(lses, axis=1))

    scores(0)
    for k in range(len(batches)):
        if k + 1 < len(batches):
            scores(k + 1)
        values(k, *softmax(k))


def _dil_attn_nat_body(q_ref, kc_ref, kh_ref, vc_ref, vh_ref, o_ref, l_ref, s_scr, p_scr):
    i = pl.program_id(1)
    w = ATTN_BLOCK
    pos_of = lambda idx: idx
    n_blk = q_ref.shape[0] // w
    batches = []
    for j0 in range(0, n_blk, ATTN_BATCH):
        blocks = []
        for j in range(j0, j0 + ATTN_BATCH):
            rs = slice(j * w, (j + 1) * w)
            if j == 0:
                kcat = jnp.concatenate([kh_ref[...], kc_ref[rs, :]], axis=0)
                vcat = jnp.concatenate([vh_ref[...], vc_ref[rs, :]], axis=0)
                has_prev = i > 0
            else:
                kcat, vcat, has_prev = kc_ref[(j - 1) * w:(j + 1) * w, :], vc_ref[(j - 1) * w:(j + 1) * w, :], None

            def write(o, lse, rs=rs):
                for half in range(ATTN_GROUP_WIDTH // LANES):
                    ls = slice(half * LANES, (half + 1) * LANES)
                    o_ref[half, rs, :] = o[:, ls]
                    l_ref[half, rs, :] = lse[:, ls]

            blocks.append((q_ref[rs, :], kcat, vcat, has_prev, write))
        batches.append(blocks)
    _attn_tile(batches, pos_of, s_scr, p_scr)


def _dil_attn_res_body(q_ref, kc_ref, kh_ref, vc_ref, vh_ref, o_ref, l_ref, s_scr, p_scr, *, dilation):
    i = pl.program_id(1)
    w = ATTN_BLOCK
    nq = MAX_DILATION // dilation
    rr = w // nq
    shift = rr.bit_length() - 1
    pos_of = lambda idx: nq * (idx & (rr - 1)) + (idx >> shift)

    def slabs(ref, r, lo):
        return [ref[q * dilation + r, lo:lo + rr, :] for q in range(nq)]

    todo = [(r, b) for r in range(dilation) for b in range(nq)]
    batches = []
    for j0 in range(0, len(todo), ATTN_BATCH):
        blocks = []
        for r, b in todo[j0:j0 + ATTN_BATCH]:
            if b == 0:
                kprev = [kh_ref[q * dilation + r] for q in range(nq)]
                vprev = [vh_ref[q * dilation + r] for q in range(nq)]
                has_prev = i > 0
            else:
                kprev, vprev, has_prev = slabs(kc_ref, r, (b - 1) * rr), slabs(vc_ref, r, (b - 1) * rr), None
            kcat = jnp.concatenate(kprev + slabs(kc_ref, r, b * rr), axis=0)
            vcat = jnp.concatenate(vprev + slabs(vc_ref, r, b * rr), axis=0)
            q = jnp.concatenate(slabs(q_ref, r, b * rr), axis=0)

            def write(o, lse, r=r, b=b):
                for qq in range(nq):
                    dst = pl.ds(MAX_DILATION * b * rr + qq * dilation + r, rr, stride=MAX_DILATION)
                    for half in range(ATTN_GROUP_WIDTH // LANES):
                        ls = slice(half * LANES, (half + 1) * LANES)
                        o_ref[half, dst, :] = o[qq * rr:(qq + 1) * rr, ls]
                        l_ref[half, dst, :] = lse[qq * rr:(qq + 1) * rr, ls]

            blocks.append((q, kcat, vcat, has_prev, write))
        batches.append(blocks)
    _attn_tile(batches, pos_of, s_scr, p_scr)


def _dil_attn(qkv, col0, dilation, bsz, seq):
    gw, w = ATTN_GROUP_WIDTH, ATTN_BLOCK
    nt = seq // ATTN_TILE
    n_sc = ATTN_TILE // w * ATTN_HEADS_PER_GROUP
    scratch = [pltpu.VMEM((n_sc, w, 2 * w), F32), pltpu.VMEM((n_sc, w, 2 * w), BF16)]
    out_spec = pl.BlockSpec((gw // LANES, ATTN_TILE, LANES), lambda b, i: (0, b * nt + i, 0))
    out_shape = [jax.ShapeDtypeStruct((gw // LANES, bsz * seq, LANES), F32)] * 2
    if dilation == 1:
        bpt = ATTN_TILE // w
        cur = lambda c: pl.BlockSpec((ATTN_TILE, gw), lambda b, i: (b * nt + i, c))
        halo = lambda c: pl.BlockSpec((w, gw), lambda b, i: (jnp.maximum((b * nt + i) * bpt - 1, 0), c))
        body = _dil_attn_nat_body
    else:
        rr = w // (MAX_DILATION // dilation)
        bpt = w // rr
        cur = lambda c: pl.BlockSpec((None, MAX_DILATION, w, gw), lambda b, i: (b, 0, i, c))
        halo = lambda c: pl.BlockSpec((None, MAX_DILATION, rr, gw),
                                      lambda b, i: (b, 0, jnp.maximum(i * bpt - 1, 0), c))
        body = functools.partial(_dil_attn_res_body, dilation=dilation)
    return pl.pallas_call(
        body,
        grid=(bsz, nt),
        in_specs=[cur(col0), cur(col0 + 1), halo(col0 + 1), cur(col0 + 2), halo(col0 + 2)],
        out_specs=[out_spec, out_spec],
        out_shape=out_shape,
        scratch_shapes=scratch,
        compiler_params=_params(2),
        name=f"dil_attn_d{dilation}",
    )(qkv, qkv, qkv, qkv, qkv)


def _mem_kv_body(mem_ref, g_ref, w_ref, k_ref, v_ref):
    n = _rms(mem_ref[...], g_ref[...]).astype(BF16)
    kv = _dot(n, w_ref[...])
    k_ref[...] = kv[:, :MEM_WIDTH].astype(BF16)
    v_ref[...] = kv[:, MEM_WIDTH:].astype(BF16)


def _mem_kv(mem, g, w_kv):
    bsz, mem_len, _ = mem.shape
    blk = pl.BlockSpec((None, mem_len, MEM_WIDTH), lambda b: (b, 0, 0))
    return pl.pallas_call(
        _mem_kv_body,
        grid=(bsz,),
        in_specs=[pl.BlockSpec((None, mem_len, D_MODEL), lambda b: (b, 0, 0)),
                  _resident((1, D_MODEL)), _resident(w_kv.shape)],
        out_specs=[blk, blk],
        out_shape=[jax.ShapeDtypeStruct((bsz, mem_len, MEM_WIDTH), BF16)] * 2,
        compiler_params=_params(1),
        name="mem_kv",
    )(mem, g, w_kv)


def _merge_body(y_ref, o0_ref, o1_ref, o2_ref, l0_ref, l1_ref, l2_ref, mq_ref, mk_ref, mv_ref,
                zg_ref, x_ref, wglut_ref, bglu_ref, wssm_ref, wattn_ref, wmem_ref, wo_ref, bgate_ref,
                h_ref):
    hh = SSM_GROUP_SIZE
    yt = jnp.concatenate([y_ref[:, c * hh:(c + 1) * hh, :].reshape(SSM_WIDTH, SSM_CHUNK)
                          for c in range(y_ref.shape[1] // hh)], axis=1)
    gy = _twice_gelu_tanh(yt)
    glu = _dot(wglut_ref[...], gy.astype(BF16)) + bglu_ref[...]
    y2 = (gy * _twice_sigmoid_of_twice(glu)).T.astype(BF16)
    br_ssm = _dot(y2, wssm_ref[...])

    halves = []
    for half in range(ATTN_GROUP_WIDTH // LANES):
        l0, l1, l2 = l0_ref[half], l1_ref[half], l2_ref[half]
        mx = jnp.maximum(jnp.maximum(l0, l1), l2)
        e0, e1, e2 = jnp.exp(l0 - mx), jnp.exp(l1 - mx), jnp.exp(l2 - mx)
        halves.append((e0 * o0_ref[half] + e1 * o1_ref[half] + e2 * o2_ref[half]) / (e0 + e1 + e2))
    br_attn = _dot(jnp.concatenate(halves, axis=1).astype(BF16), wattn_ref[...])

    e = MEM_HEAD_DIM
    scale = e ** -0.5
    heads = []
    for h in range(MEM_HEADS):
        sl = slice(h * e, (h + 1) * e)
        s = _dot_nt(mq_ref[:, sl], mk_ref[:, sl]) * scale
        p = jnp.exp(s - jnp.max(s, axis=-1, keepdims=True))
        inv = 1.0 / jnp.sum(p, axis=-1, keepdims=True)
        heads.append(_dot(p.astype(BF16), mv_ref[:, sl]) * inv)
    br_mem = _dot(jnp.concatenate(heads, axis=1).astype(BF16), wmem_ref[...])

    d = D_MODEL
    gate = lambda k: _twice_sigmoid_of_twice(
        zg_ref[:, k * d:(k + 1) * d].astype(F32) + bgate_ref[:, k * d:(k + 1) * d])
    merged = gate(0) * br_ssm + gate(1) * br_attn + gate(2) * br_mem
    h_ref[...] = x_ref[...] + _dot(merged.astype(BF16), wo_ref[...])


def _merge(y3, os, ls, mq, mk, mv, zg, x2, w_glu_t, b_glu, w_ssm, w_attn, w_mem, w_o, b_gate,
           bsz, seq, tm):
    nt = seq // tm
    cpt = tm // SSM_CHUNK
    mem_len = mk.shape[1]
    row = lambda b, i: (b * nt + i, 0)
    tok = lambda width: pl.BlockSpec((tm, width), row)
    per_batch = pl.BlockSpec((None, mem_len, MEM_WIDTH), lambda b, i: (b, 0, 0))
    weights = [w_glu_t, b_glu, w_ssm, w_attn, w_mem, w_o, b_gate]
    return pl.pallas_call(
        _merge_body,
        grid=(bsz, nt),
        in_specs=[pl.BlockSpec((SSM_GROUPS, cpt * SSM_GROUP_SIZE, SSM_CHUNK), lambda b, i: (0, b * nt + i, 0))]
        + [pl.BlockSpec((ATTN_GROUP_WIDTH // LANES, tm, LANES), lambda b, i: (0, b * nt + i, 0))] * 6
        + [tok(MEM_WIDTH), per_batch, per_batch, tok(N_BRANCHES * D_MODEL), tok(D_MODEL)]
        + [_resident(w.shape) for w in weights],
        out_specs=tok(D_MODEL),
        out_shape=jax.ShapeDtypeStruct((bsz * seq, D_MODEL), F32),
        compiler_params=_params(2),
        name="merge",
    )(y3, *os, *ls, mq, mk, mv, zg, x2, *weights)


def _mlp_body(h_ref, g2_ref, wup_ref, wdown_ref, gf_ref, out_ref, *, final_norm, ff_chunk):
    h = h_ref[...]
    n2 = _rms(h, g2_ref[...]).astype(BF16)
    acc = h
    for c in range(0, D_FF, ff_chunk):
        a = jnp.maximum(_dot(n2, wup_ref[:, c:c + ff_chunk]), 0.0)
        acc = acc + _dot((a * a).astype(BF16), wdown_ref[c:c + ff_chunk, :])
    out_ref[...] = _rms(acc, gf_ref[...]) if final_norm else acc


def _mlp(h, g2, w_up, w_down, gf, final_norm, tm):
    tokens = h.shape[0]
    row = lambda i: (i, 0)
    return pl.pallas_call(
        functools.partial(_mlp_body, final_norm=final_norm, ff_chunk=D_MODEL),
        grid=(tokens // tm,),
        in_specs=[pl.BlockSpec((tm, D_MODEL), row), _resident((1, D_MODEL)),
                  _resident(w_up.shape), _resident(w_down.shape), _resident((1, D_MODEL))],
        out_specs=pl.BlockSpec((tm, D_MODEL), row),
        out_shape=jax.ShapeDtypeStruct((tokens, D_MODEL), F32),
        compiler_params=_params(1),
        name="mlp",
    )(h, g2, w_up, w_down, gf)


def kernel(x, mem, norm1_g, mem_norm_g, w_in, b_gate, ssm_lambda_re, ssm_lambda_im, ssm_log_dt, ssm_b_re, ssm_b_im, ssm_c_re, ssm_c_im, ssm_d, w_glu, b_glu, w_ssm_br, w_attn_br, w_mem_kv, w_mem_br, w_o, norm2_g, w_up, w_down, final_g):
    bsz, seq, d_model = x.shape
    depth = w_in.shape[0]
    assert d_model == D_MODEL and seq % ATTN_TILE == 0
    assert tuple(w // d for w, d in ATTN_PATTERNS) == (ATTN_BLOCK,) * 3
    assert tuple(d for _, d in ATTN_PATTERNS) == (1, 4, 16)
    tm = 512
    h = x.reshape(bsz * seq, D_MODEL)
    row = lambda v: v.reshape(1, -1).astype(F32)
    aw, gw = ATTN_WIDTH, ATTN_GROUP_WIDTH
    o_q, o_k, o_v = SSM_WIDTH, SSM_WIDTH + aw, SSM_WIDTH + 2 * aw
    o_mq = SSM_WIDTH + 3 * aw
    o_g = o_mq + MEM_WIDTH
    for i in range(depth):
        wi = w_in[i]
        cols = lambda g: [wi[:, o + g * gw:o + (g + 1) * gw] for o in (o_q, o_k, o_v)]
        w_qkv0 = jnp.concatenate(cols(0), axis=1).astype(BF16)
        w_qkvd = jnp.concatenate(cols(1) + cols(2), axis=1).astype(BF16)
        u3, qkv0, qkvd, mq, zg = _in_proj(
            h, row(norm1_g[i]), wi[:, :SSM_WIDTH].T.astype(BF16), w_qkv0, w_qkvd,
            wi[:, o_mq:o_g].astype(BF16), (0.5 * wi[:, o_g:]).astype(BF16), bsz, seq, tm)

        ops = _ssm_operators(ssm_lambda_re[i], ssm_lambda_im[i], ssm_log_dt[i], ssm_b_re[i],
                             ssm_b_im[i], ssm_c_re[i], ssm_c_im[i], ssm_d[i])
        y3 = _ssm(u3, *ops, bsz)

        outs = [_dil_attn(qkv0, 0, 1, bsz, seq), _dil_attn(qkvd, 0, 4, bsz, seq),
                _dil_attn(qkvd, 3, 16, bsz, seq)]
        os, ls = [o for o, _ in outs], [l for _, l in outs]

        mk, mv = _mem_kv(mem, row(mem_norm_g[i]), w_mem_kv[i].astype(BF16))

        h = _merge(y3, os, ls, mq, mk, mv, zg, h,
                   (0.25 * w_glu[i]).T.astype(BF16), (0.5 * b_glu[i]).reshape(-1, 1).astype(F32),
                   (0.125 * w_ssm_br[i]).astype(BF16), (0.5 * w_attn_br[i]).astype(BF16),
                   (0.5 * w_mem_br[i]).astype(BF16), w_o[i].astype(BF16), row(0.5 * b_gate[i]),
                   bsz, seq, tm)

        h = _mlp(h, row(norm2_g[i]), w_up[i].astype(BF16), w_down[i].astype(BF16), row(final_g),
                 final_norm=(i == depth - 1), tm=tm)
    return h.reshape(bsz, seq, D_MODEL)
```

```python
import functools
import math

import jax
import jax.numpy as jnp
from jax import lax
from jax.experimental import pallas as pl
from jax.experimental.pallas import tpu as pltpu

D_MODEL = 1024
SSM_GROUP_SIZE = 16
SSM_GROUPS = 32
SSM_WIDTH = SSM_GROUP_SIZE * SSM_GROUPS
SSM_STATE = 64
SSM_CHUNK = 128
SSM_PANEL = 2
ATTN_PATTERNS = ((128, 1), (512, 4), (2048, 16))
ATTN_HEADS_PER_GROUP = 4
ATTN_HEAD_DIM = 64
ATTN_GROUP_WIDTH = ATTN_HEADS_PER_GROUP * ATTN_HEAD_DIM
ATTN_WIDTH = ATTN_GROUP_WIDTH * len(ATTN_PATTERNS)
ATTN_BLOCK = 128
MAX_DILATION = 16
ATTN_TILE = ATTN_BLOCK * MAX_DILATION
ATTN_BATCH = 1
MEM_HEADS = 4
MEM_HEAD_DIM = 128
MEM_WIDTH = MEM_HEADS * MEM_HEAD_DIM
N_BRANCHES = 3
D_FF = 4 * D_MODEL
RMS_EPS = 1e-6
NEG_INF = -1e30

LANES = 128

VMEM_LIMIT_BYTES = 56 * 1024 * 1024

F32 = jnp.float32
BF16 = jnp.bfloat16


def _params(n_axes):
    return pltpu.CompilerParams(
        dimension_semantics=("arbitrary",) * n_axes, vmem_limit_bytes=VMEM_LIMIT_BYTES)


def _resident(shape):
    nd = len(shape)
    return pl.BlockSpec(shape, lambda *_: (0,) * nd, pipeline_mode=pl.Buffered(1))


def _rms(x, g):
    ms = jnp.mean(x * x, axis=-1, keepdims=True)
    return x * lax.rsqrt(ms + RMS_EPS) * g


def _twice_sigmoid_of_twice(x):
    return 1.0 + jnp.tanh(x)


def _twice_gelu_tanh(x):
    c = math.sqrt(2.0 / math.pi)
    return x * (1.0 + jnp.tanh(x * (c + (c * 0.044715) * (x * x))))


def _dot(a, b):
    return jnp.dot(a, b, preferred_element_type=F32)


def _dot_nt(a, b):
    return lax.dot_general(a, b, (((1,), (1,)), ((), ())), preferred_element_type=F32)


def _in_proj_body(x_ref, g_ref, wut_ref, wqkv0_ref, wqkvd_ref, u_ref, qkv0_ref, qkvd_ref, n_scr):
    tm = x_ref.shape[0]
    nf = _rms(x_ref[...], g_ref[...])
    n = nf.astype(BF16)
    for c in range(D_MODEL // LANES):
        n_scr[c] = nf[:, c * LANES:(c + 1) * LANES]
    ut = _dot_nt(wut_ref[...], n)
    for c in range(tm // SSM_CHUNK):
        u_ref[:, c * SSM_GROUP_SIZE:(c + 1) * SSM_GROUP_SIZE, :] = (
            ut[:, c * SSM_CHUNK:(c + 1) * SSM_CHUNK].reshape(SSM_GROUPS, SSM_GROUP_SIZE, SSM_CHUNK))
    qkv0_ref[...] = _dot(n, wqkv0_ref[...]).astype(BF16)
    rows = tm // MAX_DILATION
    n_perm = jnp.concatenate(
        [jnp.concatenate([n_scr[c, pl.ds(r, rows, stride=MAX_DILATION), :]
                          for c in range(D_MODEL // LANES)], axis=1)
         for r in range(MAX_DILATION)], axis=0)
    zp = _dot(n_perm.astype(BF16), wqkvd_ref[...]).astype(BF16)
    for r in range(MAX_DILATION):
        qkvd_ref[r] = zp[r * rows:(r + 1) * rows]


def _in_proj(x2, g, w_ut, w_qkv0, w_qkvd, bsz, seq, tm):
    tokens = bsz * seq
    nt = seq // tm
    cpt = tm // SSM_CHUNK
    n_chunks = seq // SSM_CHUNK
    wd = w_qkvd.shape[1]
    row = lambda b, i: (b * nt + i, 0)
    return pl.pallas_call(
        _in_proj_body,
        grid=(bsz, nt),
        in_specs=[
            pl.BlockSpec((tm, D_MODEL), row),
            _resident((1, D_MODEL)),
            _resident(w_ut.shape), _resident(w_qkv0.shape), _resident(w_qkvd.shape),
        ],
        out_specs=[
            pl.BlockSpec((SSM_GROUPS, cpt * SSM_GROUP_SIZE, SSM_CHUNK), lambda b, i: (0, b * nt + i, 0)),
            pl.BlockSpec((tm, ATTN_WIDTH), row),
            pl.BlockSpec((None, MAX_DILATION, tm // MAX_DILATION, wd), lambda b, i: (b, 0, i, 0)),
        ],
        out_shape=[
            jax.ShapeDtypeStruct((SSM_GROUPS, bsz * n_chunks * SSM_GROUP_SIZE, SSM_CHUNK), F32),
            jax.ShapeDtypeStruct((tokens, ATTN_WIDTH), BF16),
            jax.ShapeDtypeStruct((bsz, MAX_DILATION, seq // MAX_DILATION, wd), BF16),
        ],
        scratch_shapes=[pltpu.VMEM((D_MODEL // LANES, tm, LANES), F32)],
        compiler_params=_params(2),
        name="in_proj",
    )(x2, g, w_ut, w_qkv0, w_qkvd)


def _ssm_body(u_ref, pw_ref, bc_ref, dtap_ref, a_ref, y_ref,
              k_scr, t_scr, m_scr, nt_scr, z_scr, sp_scr, *, bsz):
    hh = SSM_GROUP_SIZE
    cc = u_ref.shape[1]
    rows = u_ref.shape[0] // hh
    n_chunks = rows // bsz
    half = 2 * SSM_STATE

    q_re, q_im, r_re, r_im = pw_ref[0], pw_ref[1], pw_ref[2], pw_ref[3]
    b_re, b_im, c_re, c_im = bc_ref[0], bc_ref[1], bc_ref[2], bc_ref[3]
    w_re, w_im = [], []
    for h in range(hh):
        rs = slice(h * cc, (h + 1) * cc)
        bre, bim, cre, cim = b_re[h:h + 1], b_im[h:h + 1], c_re[h:h + 1], c_im[h:h + 1]
        m_scr[rs, 0:half] = (q_re * bre - q_im * bim).astype(BF16)
        m_scr[rs, half:2 * half] = (q_re * bim + q_im * bre).astype(BF16)
        nt_scr[rs, 0:half] = (r_re * cre - r_im * cim).astype(BF16)
        nt_scr[rs, half:2 * half] = (-(r_im * cre + r_re * cim)).astype(BF16)
        w_re.append(c_re * bre - c_im * bim)
        w_im.append(c_re * bim + c_im * bre)
    hp = lax.Precision.HIGHEST
    k_scr[...] = (jnp.dot(jnp.concatenate(w_re, axis=0), pw_ref[4], precision=hp, preferred_element_type=F32)
                  - jnp.dot(jnp.concatenate(w_im, axis=0), pw_ref[5], precision=hp, preferred_element_type=F32)
                  + dtap_ref[...])

    srow = lax.broadcasted_iota(jnp.int32, (cc, cc), 0)
    tcol = lax.broadcasted_iota(jnp.int32, (cc, cc), 1)
    causal = tcol >= srow

    x = jnp.concatenate([u_ref[pl.ds(h, rows, stride=hh), :].astype(BF16) for h in range(hh)],
                        axis=1)
    z_scr[...] = _dot(x, m_scr[...]).reshape(bsz, n_chunks, 2 * half)
    ar = a_ref[0:1, :].reshape(1, 1, half)
    ai = a_ref[1:2, :].reshape(1, 1, half)
    sr = jnp.zeros((bsz, 1, half), F32)
    si = jnp.zeros((bsz, 1, half), F32)
    for c in range(n_chunks):
        sp_scr[:, c:c + 1, 0:half] = sr
        sp_scr[:, c:c + 1, half:2 * half] = si
        zr = z_scr[:, c:c + 1, 0:half]
        zi = z_scr[:, c:c + 1, half:2 * half]
        sr, si = ar * sr - ai * si + zr, ar * si + ai * sr + zi
    sp = sp_scr[...].reshape(rows, 2 * half).astype(BF16)

    for p0 in range(0, hh, SSM_PANEL):
        cols = slice(p0 * cc, (p0 + SSM_PANEL) * cc)
        for h in range(hh):
            for hp in range(p0, p0 + SSM_PANEL):
                kb = jnp.broadcast_to(k_scr[h * hh + hp:h * hh + hp + 1, :], (cc, cc))
                blk = pltpu.roll(kb, 0, 1, stride=1, stride_axis=0)
                t_scr[h * cc:(h + 1) * cc, hp * cc:(hp + 1) * cc] = (
                    jnp.where(causal, blk, 0.0).astype(BF16))
        y = _dot(x, t_scr[:, cols]) + _dot_nt(sp, nt_scr[cols, :])
        for hp in range(p0, p0 + SSM_PANEL):
            y_ref[pl.ds(hp, rows, stride=hh), :] = y[:, (hp - p0) * cc:(hp - p0 + 1) * cc]


def _ssm(u3, pw, bc, dtap, a_pow, bsz):
    hh = SSM_GROUP_SIZE
    _, rows16, cc = u3.shape
    n_chunks = rows16 // hh // bsz
    width = hh * cc
    half = 2 * SSM_STATE
    blk = pl.BlockSpec((None, rows16, cc), lambda g: (g, 0, 0))
    per_group = lambda a: pl.BlockSpec((None,) + a.shape[1:], lambda g: (g,) + (0,) * (a.ndim - 1))
    return pl.pallas_call(
        functools.partial(_ssm_body, bsz=bsz),
        grid=(SSM_GROUPS,),
        in_specs=[blk, per_group(pw), per_group(bc), per_group(dtap), per_group(a_pow)],
        out_specs=blk,
        out_shape=jax.ShapeDtypeStruct(u3.shape, F32),
        scratch_shapes=[
            pltpu.VMEM((hh * hh, cc), F32),
            pltpu.VMEM((width, width), BF16),
            pltpu.VMEM((width, 2 * half), BF16),
            pltpu.VMEM((width, 2 * half), BF16),
            pltpu.VMEM((bsz, n_chunks, 2 * half), F32),
            pltpu.VMEM((bsz, n_chunks, 2 * half), F32),
        ],
        compiler_params=_params(1),
        name="ssm",
    )(u3, pw, bc, dtap, a_pow)


def _ssm_operators(lam_re, lam_im, log_dt, b_re, b_im, c_re, c_im, d_skip):
    cc, pp, hh, gg = SSM_CHUNK, SSM_STATE, SSM_GROUP_SIZE, SSM_GROUPS
    lr, li = lam_re.astype(F32), lam_im.astype(F32)
    dt = jnp.exp(log_dt.astype(F32))[:, None]
    mag = jnp.exp(lr * dt)
    a_re, a_im = mag * jnp.cos(li * dt), mag * jnp.sin(li * dt)
    nr, ni = a_re - 1.0, a_im
    den = lr * lr + li * li
    coef_re = (nr * lr + ni * li) / den
    coef_im = (ni * lr - nr * li) / den
    br_, bi_ = b_re.astype(F32), b_im.astype(F32)
    bb_re = coef_re[..., None] * br_ - coef_im[..., None] * bi_
    bb_im = coef_re[..., None] * bi_ + coef_im[..., None] * br_
    cr, ci = c_re.astype(F32), c_im.astype(F32)
    dd = d_skip.astype(F32)

    k = jnp.arange(cc + 1, dtype=F32)[:, None, None]
    pmag = jnp.exp(k * (lr * dt)[None])
    pang = k * (li * dt)[None]
    p_re, p_im = pmag * jnp.cos(pang), pmag * jnp.sin(pang)

    lane_pad = lambda v: jnp.pad(v, [(0, 0)] * (v.ndim - 1) + [(0, pp)])
    by_group = lambda v: lane_pad(v.transpose(1, 0, 2))
    flip = cc - 1 - jnp.arange(cc)
    pt = lambda v: jnp.pad(v[:cc].transpose(1, 2, 0), ((0, 0), (0, pp), (0, 0)))
    pw = jnp.stack([by_group(p_re[flip]), by_group(p_im[flip]),
                    by_group(p_re[1:cc + 1]), by_group(p_im[1:cc + 1]),
                    pt(p_re), pt(p_im)], axis=1)
    bc = jnp.stack([lane_pad(bb_re.transpose(0, 2, 1)), lane_pad(bb_im.transpose(0, 2, 1)),
                    lane_pad(cr), lane_pad(ci)], axis=1)
    diag = jnp.arange(hh) * (hh + 1)
    dtap = jnp.zeros((gg, hh * hh, cc), F32).at[:, diag, 0].set(dd)
    a_pow = jnp.zeros((gg, 8, 2 * pp), F32)
    a_pow = a_pow.at[:, 0].set(lane_pad(p_re[cc])).at[:, 1].set(lane_pad(p_im[cc]))
    return pw, bc, dtap, a_pow


def _attn_tile(batches, pos_of, s_scr, p_scr):
    w, e, nh = ATTN_BLOCK, ATTN_HEAD_DIM, ATTN_HEADS_PER_GROUP
    lane = lax.broadcasted_iota(jnp.int32, (w, 2 * w), 1)
    pkey = pos_of(lane & (w - 1))
    pqry = pos_of(lax.broadcasted_iota(jnp.int32, (w, 2 * w), 0))
    mask = jnp.logical_or(jnp.logical_and(lane < w, pkey >= pqry),
                          jnp.logical_and(lane >= w, pkey <= pqry))
    head_of_lane = lane // e
    scale = e ** -0.5
    slot0 = [0]
    for blocks in batches:
        slot0.append(slot0[-1] + len(blocks) * nh)

    def scores(k):
        for bi, (q, kcat, _, has_prev, _) in enumerate(batches[k]):
            qs = q * scale
            qm = jnp.concatenate([jnp.where(head_of_lane == h, qs, 0) for h in range(nh)], axis=0)
            s = _dot_nt(qm, kcat).reshape(nh, w, 2 * w)
            if has_prev is not None:
                s = jnp.where(jnp.logical_or(lane >= w, has_prev)[None], s, NEG_INF)
            s_scr[slot0[k] + bi * nh:slot0[k] + (bi + 1) * nh] = s

    def softmax(k):
        s_all = jnp.where(mask[None], s_scr[slot0[k]:slot0[k + 1]], NEG_INF)
        mx = jnp.max(s_all, axis=-1, keepdims=True)
        p = jnp.exp(s_all - mx)
        den = jnp.sum(p, axis=-1, keepdims=True)
        p_scr[slot0[k]:slot0[k + 1]] = p.astype(BF16)
        return mx + jnp.log(den), 1.0 / den

    def values(k, lse, inv):
        for bi, (_, _, vcat, _, write) in enumerate(batches[k]):
            lo = slot0[k] + bi * nh
            pv = _dot(p_scr[lo:lo + nh].reshape(nh * w, 2 * w), vcat).reshape(nh, w, nh * e)
            outs = [pv[h][:, h * e:(h + 1) * e] * inv[bi * nh + h] for h in range(nh)]
            lses = [jnp.broadcast_to(lse[bi * nh + h], (w, e)) for h in range(nh)]
            write(jnp.concatenate(outs, axis=1), jnp.concatenate(lses, axis=1))

    scores(0)
    for k in range(len(batches)):
        if k + 1 < len(batches):
            scores(k + 1)
        values(k, *softmax(k))


def _dil_attn_nat_body(q_ref, kc_ref, kh_ref, vc_ref, vh_ref, o_ref, l_ref, s_scr, p_scr):
    i = pl.program_id(1)
    w = ATTN_BLOCK
    pos_of = lambda idx: idx
    n_blk = q_ref.shape[0] // w
    batches = []
    for j0 in range(0, n_blk, ATTN_BATCH):
        blocks = []
        for j in range(j0, j0 + ATTN_BATCH):
            rs = slice(j * w, (j + 1) * w)
            if j == 0:
                kcat = jnp.concatenate([kh_ref[...], kc_ref[rs, :]], axis=0)
                vcat = jnp.concatenate([vh_ref[...], vc_ref[rs, :]], axis=0)
                has_prev = i > 0
            else:
                kcat, vcat, has_prev = kc_ref[(j - 1) * w:(j + 1) * w, :], vc_ref[(j - 1) * w:(j + 1) * w, :], None

            def write(o, lse, rs=rs):
                for half in range(ATTN_GROUP_WIDTH // LANES):
                    ls = slice(half * LANES, (half + 1) * LANES)
                    o_ref[half, rs, :] = o[:, ls]
                    l_ref[half, rs, :] = lse[:, ls]

            blocks.append((q_ref[rs, :], kcat, vcat, has_prev, write))
        batches.append(blocks)
    _attn_tile(batches, pos_of, s_scr, p_scr)


def _dil_attn_res_body(q_ref, kc_ref, kh_ref, vc_ref, vh_ref, o_ref, l_ref, s_scr, p_scr, *, dilation):
    i = pl.program_id(1)
    w = ATTN_BLOCK
    nq = MAX_DILATION // dilation
    rr = w // nq
    shift = rr.bit_length() - 1
    pos_of = lambda idx: nq * (idx & (rr - 1)) + (idx >> shift)

    def slabs(ref, r, lo):
        return [ref[q * dilation + r, lo:lo + rr, :] for q in range(nq)]

    todo = [(r, b) for r in range(dilation) for b in range(nq)]
    batches = []
    for j0 in range(0, len(todo), ATTN_BATCH):
        blocks = []
        for r, b in todo[j0:j0 + ATTN_BATCH]:
            if b == 0:
                kprev = [kh_ref[q * dilation + r] for q in range(nq)]
                vprev = [vh_ref[q * dilation + r] for q in range(nq)]
                has_prev = i > 0
            else:
                kprev, vprev, has_prev = slabs(kc_ref, r, (b - 1) * rr), slabs(vc_ref, r, (b - 1) * rr), None
            kcat = jnp.concatenate(kprev + slabs(kc_ref, r, b * rr), axis=0)
            vcat = jnp.concatenate(vprev + slabs(vc_ref, r, b * rr), axis=0)
            q = jnp.concatenate(slabs(q_ref, r, b * rr), axis=0)

            def write(o, lse, r=r, b=b):
                for qq in range(nq):
                    dst = pl.ds(MAX_DILATION * b * rr + qq * dilation + r, rr, stride=MAX_DILATION)
                    for half in range(ATTN_GROUP_WIDTH // LANES):
                        ls = slice(half * LANES, (half + 1) * LANES)
                        o_ref[half, dst, :] = o[qq * rr:(qq + 1) * rr, ls]
                        l_ref[half, dst, :] = lse[qq * rr:(qq + 1) * rr, ls]

            blocks.append((q, kcat, vcat, has_prev, write))
        batches.append(blocks)
    _attn_tile(batches, pos_of, s_scr, p_scr)


def _dil_attn(qkv, col0, dilation, bsz, seq):
    gw, w = ATTN_GROUP_WIDTH, ATTN_BLOCK
    nt = seq // ATTN_TILE
    n_sc = ATTN_TILE // w * ATTN_HEADS_PER_GROUP
    scratch = [pltpu.VMEM((n_sc, w, 2 * w), F32), pltpu.VMEM((n_sc, w, 2 * w), BF16)]
    out_spec = pl.BlockSpec((gw // LANES, ATTN_TILE, LANES), lambda b, i: (0, b * nt + i, 0))
    out_shape = [jax.ShapeDtypeStruct((gw // LANES, bsz * seq, LANES), F32)] * 2
    if dilation == 1:
        bpt = ATTN_TILE // w
        cur = lambda c: pl.BlockSpec((ATTN_TILE, gw), lambda b, i: (b * nt + i, c))
        halo = lambda c: pl.BlockSpec((w, gw), lambda b, i: (jnp.maximum((b * nt + i) * bpt - 1, 0), c))
        body = _dil_attn_nat_body
    else:
        rr = w // (MAX_DILATION // dilation)
        bpt = w // rr
        cur = lambda c: pl.BlockSpec((None, MAX_DILATION, w, gw), lambda b, i: (b, 0, i, c))
        halo = lambda c: pl.BlockSpec((None, MAX_DILATION, rr, gw),
                                      lambda b, i: (b, 0, jnp.maximum(i * bpt - 1, 0), c))
        body = functools.partial(_dil_attn_res_body, dilation=dilation)
    return pl.pallas_call(
        body,
        grid=(bsz, nt),
        in_specs=[cur(col0), cur(col0 + 1), halo(col0 + 1), cur(col0 + 2), halo(col0 + 2)],
        out_specs=[out_spec, out_spec],
        out_shape=out_shape,
        scratch_shapes=scratch,
        compiler_params=_params(2),
        name=f"dil_attn_d{dilation}",
    )(qkv, qkv, qkv, qkv, qkv)


def _mem_kv_body(mem_ref, g_ref, w_ref, k_ref, v_ref):
    n = _rms(mem_ref[...], g_ref[...]).astype(BF16)
    kv = _dot(n, w_ref[...])
    k_ref[...] = kv[:, :MEM_WIDTH].astype(BF16)
    v_ref[...] = kv[:, MEM_WIDTH:].astype(BF16)


def _mem_kv(mem, g, w_kv):
    bsz, mem_len, _ = mem.shape
    blk = pl.BlockSpec((None, mem_len, MEM_WIDTH), lambda b: (b, 0, 0))
    return pl.pallas_call(
        _mem_kv_body,
        grid=(bsz,),
        in_specs=[pl.BlockSpec((None, mem_len, D_MODEL), lambda b: (b, 0, 0)),
                  _resident((1, D_MODEL)), _resident(w_kv.shape)],
        out_specs=[blk, blk],
        out_shape=[jax.ShapeDtypeStruct((bsz, mem_len, MEM_WIDTH), BF16)] * 2,
        compiler_params=_params(1),
        name="mem_kv",
    )(mem, g, w_kv)


def _merge_body(y_ref, o0_ref, o1_ref, o2_ref, l0_ref, l1_ref, l2_ref, mk_ref, mv_ref, x_ref,
                g1_ref, wmq_ref, wg_ref, wglut_ref, bglu_ref, wssm_ref, wattn_ref, wmem_ref, wo_ref,
                bgate_ref, h_ref):
    n = _rms(x_ref[...], g1_ref[...]).astype(BF16)

    hh = SSM_GROUP_SIZE
    yt = jnp.concatenate([y_ref[:, c * hh:(c + 1) * hh, :].reshape(SSM_WIDTH, SSM_CHUNK)
                          for c in range(y_ref.shape[1] // hh)], axis=1)
    gy = _twice_gelu_tanh(yt)
    glu = _dot(wglut_ref[...], gy.astype(BF16)) + bglu_ref[...]
    y2 = (gy * _twice_sigmoid_of_twice(glu)).T.astype(BF16)
    br_ssm = _dot(y2, wssm_ref[...])

    halves = []
    for half in range(ATTN_GROUP_WIDTH // LANES):
        l0, l1, l2 = l0_ref[half], l1_ref[half], l2_ref[half]
        mx = jnp.maximum(jnp.maximum(l0, l1), l2)
        e0, e1, e2 = jnp.exp(l0 - mx), jnp.exp(l1 - mx), jnp.exp(l2 - mx)
        halves.append((e0 * o0_ref[half] + e1 * o1_ref[half] + e2 * o2_ref[half]) / (e0 + e1 + e2))
    br_attn = _dot(jnp.concatenate(halves, axis=1).astype(BF16), wattn_ref[...])

    e = MEM_HEAD_DIM
    scale = e ** -0.5
    mq = _dot(n, wmq_ref[...]).astype(BF16)
    heads = []
    for h in range(MEM_HEADS):
        sl = slice(h * e, (h + 1) * e)
        s = _dot_nt(mq[:, sl], mk_ref[:, sl]) * scale
        p = jnp.exp(s - jnp.max(s, axis=-1, keepdims=True))
        inv = 1.0 / jnp.sum(p, axis=-1, keepdims=True)
        heads.append(_dot(p.astype(BF16), mv_ref[:, sl]) * inv)
    br_mem = _dot(jnp.concatenate(heads, axis=1).astype(BF16), wmem_ref[...])

    d = D_MODEL
    gate = lambda k: _twice_sigmoid_of_twice(
        _dot(n, wg_ref[:, k * d:(k + 1) * d]) + bgate_ref[:, k * d:(k + 1) * d])
    merged = gate(0) * br_ssm + gate(1) * br_attn + gate(2) * br_mem
    h_ref[...] = x_ref[...] + _dot(merged.astype(BF16), wo_ref[...])


def _merge(y3, os, ls, mk, mv, x2, weights, bsz, seq, tm):
    nt = seq // tm
    cpt = tm // SSM_CHUNK
    mem_len = mk.shape[1]
    row = lambda b, i: (b * nt + i, 0)
    tok = lambda width: pl.BlockSpec((tm, width), row)
    per_batch = pl.BlockSpec((None, mem_len, MEM_WIDTH), lambda b, i: (b, 0, 0))
    return pl.pallas_call(
        _merge_body,
        grid=(bsz, nt),
        in_specs=[pl.BlockSpec((SSM_GROUPS, cpt * SSM_GROUP_SIZE, SSM_CHUNK), lambda b, i: (0, b * nt + i, 0))]
        + [pl.BlockSpec((ATTN_GROUP_WIDTH // LANES, tm, LANES), lambda b, i: (0, b * nt + i, 0))] * 6
        + [per_batch, per_batch, tok(D_MODEL)]
        + [_resident(w.shape) for w in weights],
        out_specs=tok(D_MODEL),
        out_shape=jax.ShapeDtypeStruct((bsz * seq, D_MODEL), F32),
        compiler_params=_params(2),
        name="merge",
    )(y3, *os, *ls, mk, mv, x2, *weights)


def _mlp_body(h_ref, g2_ref, wup_ref, wdown_ref, gf_ref, out_ref, *, final_norm, ff_chunk):
    h = h_ref[...]
    n2 = _rms(h, g2_ref[...]).astype(BF16)
    acc = h
    for c in range(0, D_FF, ff_chunk):
        a = jnp.maximum(_dot(n2, wup_ref[:, c:c + ff_chunk]), 0.0)
        acc = acc + _dot((a * a).astype(BF16), wdown_ref[c:c + ff_chunk, :])
    out_ref[...] = _rms(acc, gf_ref[...]) if final_norm else acc


def _mlp(h, g2, w_up, w_down, gf, final_norm, tm):
    tokens = h.shape[0]
    row = lambda i: (i, 0)
    return pl.pallas_call(
        functools.partial(_mlp_body, final_norm=final_norm, ff_chunk=D_MODEL),
        grid=(tokens // tm,),
        in_specs=[pl.BlockSpec((tm, D_MODEL), row), _resident((1, D_MODEL)),
                  _resident(w_up.shape), _resident(w_down.shape), _resident((1, D_MODEL))],
        out_specs=pl.BlockSpec((tm, D_MODEL), row),
        out_shape=jax.ShapeDtypeStruct((tokens, D_MODEL), F32),
        compiler_params=_params(1),
        name="mlp",
    )(h, g2, w_up, w_down, gf)


def kernel(x, mem, norm1_g, mem_norm_g, w_in, b_gate, ssm_lambda_re, ssm_lambda_im, ssm_log_dt, ssm_b_re, ssm_b_im, ssm_c_re, ssm_c_im, ssm_d, w_glu, b_glu, w_ssm_br, w_attn_br, w_mem_kv, w_mem_br, w_o, norm2_g, w_up, w_down, final_g):
    bsz, seq, d_model = x.shape
    depth = w_in.shape[0]
    assert d_model == D_MODEL and seq % ATTN_TILE == 0
    assert tuple(w // d for w, d in ATTN_PATTERNS) == (ATTN_BLOCK,) * 3
    assert tuple(d for _, d in ATTN_PATTERNS) == (1, 4, 16)
    tm = 512
    h = x.reshape(bsz * seq, D_MODEL)
    row = lambda v: v.reshape(1, -1).astype(F32)
    aw, gw = ATTN_WIDTH, ATTN_GROUP_WIDTH
    o_q, o_k, o_v = SSM_WIDTH, SSM_WIDTH + aw, SSM_WIDTH + 2 * aw
    o_mq = SSM_WIDTH + 3 * aw
    o_g = o_mq + MEM_WIDTH
    for i in range(depth):
        wi = w_in[i]
        cols = lambda g: [wi[:, o + g * gw:o + (g + 1) * gw] for o in (o_q, o_k, o_v)]
        w_qkv0 = jnp.concatenate(cols(0), axis=1).astype(BF16)
        w_qkvd = jnp.concatenate(cols(1) + cols(2), axis=1).astype(BF16)
        u3, qkv0, qkvd = _in_proj(
            h, row(norm1_g[i]), wi[:, :SSM_WIDTH].T.astype(BF16), w_qkv0, w_qkvd, bsz, seq, tm)

        ops = _ssm_operators(ssm_lambda_re[i], ssm_lambda_im[i], ssm_log_dt[i], ssm_b_re[i],
                             ssm_b_im[i], ssm_c_re[i], ssm_c_im[i], ssm_d[i])
        y3 = _ssm(u3, *ops, bsz)

        outs = [_dil_attn(qkv0, 0, 1, bsz, seq), _dil_attn(qkvd, 0, 4, bsz, seq),
                _dil_attn(qkvd, 3, 16, bsz, seq)]
        os, ls = [o for o, _ in outs], [l for _, l in outs]

        mk, mv = _mem_kv(mem, row(mem_norm_g[i]), w_mem_kv[i].astype(BF16))

        merge_weights = [
            row(norm1_g[i]), wi[:, o_mq:o_g].astype(BF16), (0.5 * wi[:, o_g:]).astype(BF16),
            (0.25 * w_glu[i]).T.astype(BF16), (0.5 * b_glu[i]).reshape(-1, 1).astype(F32),
            (0.125 * w_ssm_br[i]).astype(BF16), (0.5 * w_attn_br[i]).astype(BF16),
            (0.5 * w_mem_br[i]).astype(BF16), w_o[i].astype(BF16), row(0.5 * b_gate[i])]
        h = _merge(y3, os, ls, mk, mv, h, merge_weights, bsz, seq, tm)

        h = _mlp(h, row(norm2_g[i]), w_up[i].astype(BF16), w_down[i].astype(BF16), row(final_g),
                 final_norm=(i == depth - 1), tm=tm)
    return h.reshape(bsz, seq, D_MODEL)
```

```python
import functools
import math

import jax
import jax.numpy as jnp
from jax import lax
from jax.experimental import pallas as pl
from jax.experimental.pallas import tpu as pltpu

D_MODEL = 1024
SSM_GROUP_SIZE = 16
SSM_GROUPS = 32
SSM_WIDTH = SSM_GROUP_SIZE * SSM_GROUPS
SSM_STATE = 64
SSM_CHUNK = 128
SSM_PANEL = 2
ATTN_PATTERNS = ((128, 1), (512, 4), (2048, 16))
ATTN_HEADS_PER_GROUP = 4
ATTN_HEAD_DIM = 64
ATTN_GROUP_WIDTH = ATTN_HEADS_PER_GROUP * ATTN_HEAD_DIM
ATTN_WIDTH = ATTN_GROUP_WIDTH * len(ATTN_PATTERNS)
ATTN_BLOCK = 128
MAX_DILATION = 16
ATTN_TILE = ATTN_BLOCK * MAX_DILATION
ATTN_BATCH = 1
MEM_HEADS = 4
MEM_HEAD_DIM = 128
MEM_WIDTH = MEM_HEADS * MEM_HEAD_DIM
N_BRANCHES = 3
D_FF = 4 * D_MODEL
RMS_EPS = 1e-6
NEG_INF = -1e30

LANES = 128

VMEM_LIMIT_BYTES = 56 * 1024 * 1024

F32 = jnp.float32
BF16 = jnp.bfloat16


def _params(n_axes):
    return pltpu.CompilerParams(
        dimension_semantics=("arbitrary",) * n_axes, vmem_limit_bytes=VMEM_LIMIT_BYTES)


def _resident(shape):
    nd = len(shape)
    return pl.BlockSpec(shape, lambda *_: (0,) * nd, pipeline_mode=pl.Buffered(1))


def _rms(x, g):
    ms = jnp.mean(x * x, axis=-1, keepdims=True)
    return x * lax.rsqrt(ms + RMS_EPS) * g


def _twice_sigmoid_of_twice(x):
    return 1.0 + jnp.tanh(x)


def _twice_gelu_tanh(x):
    c = math.sqrt(2.0 / math.pi)
    return x * (1.0 + jnp.tanh(x * (c + (c * 0.044715) * (x * x))))


def _dot(a, b):
    return jnp.dot(a, b, preferred_element_type=F32)


def _dot_nt(a, b):
    return lax.dot_general(a, b, (((1,), (1,)), ((), ())), preferred_element_type=F32)


def _in_proj_body(x_ref, g_ref, wut_ref, wqkv0_ref, wqkvd_ref, u_ref, qkv0_ref, qkvd_ref, n_scr):
    tm = x_ref.shape[0]
    nf = _rms(x_ref[...], g_ref[...])
    n = nf.astype(BF16)
    for c in range(D_MODEL // LANES):
        n_scr[c] = nf[:, c * LANES:(c + 1) * LANES]
    ut = _dot_nt(wut_ref[...], n)
    for c in range(tm // SSM_CHUNK):
        u_ref[:, c * SSM_GROUP_SIZE:(c + 1) * SSM_GROUP_SIZE, :] = (
            ut[:, c * SSM_CHUNK:(c + 1) * SSM_CHUNK].reshape(SSM_GROUPS, SSM_GROUP_SIZE, SSM_CHUNK))
    qkv0_ref[...] = _dot(n, wqkv0_ref[...]).astype(BF16)
    rows = tm // MAX_DILATION
    n_perm = jnp.concatenate(
        [jnp.concatenate([n_scr[c, pl.ds(r, rows, stride=MAX_DILATION), :]
                          for c in range(D_MODEL // LANES)], axis=1)
         for r in range(MAX_DILATION)], axis=0)
    zp = _dot(n_perm.astype(BF16), wqkvd_ref[...]).astype(BF16)
    for r in range(MAX_DILATION):
        qkvd_ref[r] = zp[r * rows:(r + 1) * rows]


def _in_proj(x2, g, w_ut, w_qkv0, w_qkvd, bsz, seq, tm):
    tokens = bsz * seq
    nt = seq // tm
    cpt = tm // SSM_CHUNK
    n_chunks = seq // SSM_CHUNK
    wd = w_qkvd.shape[1]
    row = lambda b, i: (b * nt + i, 0)
    return pl.pallas_call(
        _in_proj_body,
        grid=(bsz, nt),
        in_specs=[
            pl.BlockSpec((tm, D_MODEL), row),
            _resident((1, D_MODEL)),
            _resident(w_ut.shape), _resident(w_qkv0.shape), _resident(w_qkvd.shape),
        ],
        out_specs=[
            pl.BlockSpec((SSM_GROUPS, cpt * SSM_GROUP_SIZE, SSM_CHUNK), lambda b, i: (0, b * nt + i, 0)),
            pl.BlockSpec((tm, ATTN_WIDTH), row),
            pl.BlockSpec((None, MAX_DILATION, tm // MAX_DILATION, wd), lambda b, i: (b, 0, i, 0)),
        ],
        out_shape=[
            jax.ShapeDtypeStruct((SSM_GROUPS, bsz * n_chunks * SSM_GROUP_SIZE, SSM_CHUNK), F32),
            jax.ShapeDtypeStruct((tokens, ATTN_WIDTH), BF16),
            jax.ShapeDtypeStruct((bsz, MAX_DILATION, seq // MAX_DILATION, wd), BF16),
        ],
        scratch_shapes=[pltpu.VMEM((D_MODEL // LANES, tm, LANES), F32)],
        compiler_params=_params(2),
        name="in_proj",
    )(x2, g, w_ut, w_qkv0, w_qkvd)


def _ssm_body(u_ref, pw_ref, bc_ref, dtap_ref, a_ref, y_ref,
              k_scr, t_scr, m_scr, nt_scr, z_scr, sp_scr, *, bsz):
    hh = SSM_GROUP_SIZE
    cc = u_ref.shape[1]
    rows = u_ref.shape[0] // hh
    n_chunks = rows // bsz
    half = 2 * SSM_STATE

    q_re, q_im, r_re, r_im = pw_ref[0], pw_ref[1], pw_ref[2], pw_ref[3]
    b_re, b_im, c_re, c_im = bc_ref[0], bc_ref[1], bc_ref[2], bc_ref[3]
    w_re, w_im = [], []
    for h in range(hh):
        rs = slice(h * cc, (h + 1) * cc)
        bre, bim, cre, cim = b_re[h:h + 1], b_im[h:h + 1], c_re[h:h + 1], c_im[h:h + 1]
        m_scr[rs, 0:half] = (q_re * bre - q_im * bim).astype(BF16)
        m_scr[rs, half:2 * half] = (q_re * bim + q_im * bre).astype(BF16)
        nt_scr[rs, 0:half] = (r_re * cre - r_im * cim).astype(BF16)
        nt_scr[rs, half:2 * half] = (-(r_im * cre + r_re * cim)).astype(BF16)
        w_re.append(c_re * bre - c_im * bim)
        w_im.append(c_re * bim + c_im * bre)
    hp = lax.Precision.HIGHEST
    k_scr[...] = (jnp.dot(jnp.concatenate(w_re, axis=0), pw_ref[4], precision=hp, preferred_element_type=F32)
                  - jnp.dot(jnp.concatenate(w_im, axis=0), pw_ref[5], precision=hp, preferred_element_type=F32)
                  + dtap_ref[...])

    srow = lax.broadcasted_iota(jnp.int32, (cc, cc), 0)
    tcol = lax.broadcasted_iota(jnp.int32, (cc, cc), 1)
    causal = tcol >= srow

    x = jnp.concatenate([u_ref[pl.ds(h, rows, stride=hh), :].astype(BF16) for h in range(hh)],
                        axis=1)
    z_scr[...] = _dot(x, m_scr[...]).reshape(bsz, n_chunks, 2 * half)
    ar = a_ref[0:1, :].reshape(1, 1, half)
    ai = a_ref[1:2, :].reshape(1, 1, half)
    sr = jnp.zeros((bsz, 1, half), F32)
    si = jnp.zeros((bsz, 1, half), F32)
    for c in range(n_chunks):
        sp_scr[:, c:c + 1, 0:half] = sr
        sp_scr[:, c:c + 1, half:2 * half] = si
        zr = z_scr[:, c:c + 1, 0:half]
        zi = z_scr[:, c:c + 1, half:2 * half]
        sr, si = ar * sr - ai * si + zr, ar * si + ai * sr + zi
    sp = sp_scr[...].reshape(rows, 2 * half).astype(BF16)

    for p0 in range(0, hh, SSM_PANEL):
        cols = slice(p0 * cc, (p0 + SSM_PANEL) * cc)
        for h in range(hh):
            for hp in range(p0, p0 + SSM_PANEL):
                kb = jnp.broadcast_to(k_scr[h * hh + hp:h * hh + hp + 1, :], (cc, cc))
                blk = pltpu.roll(kb, 0, 1, stride=1, stride_axis=0)
                t_scr[h * cc:(h + 1) * cc, hp * cc:(hp + 1) * cc] = (
                    jnp.where(causal, blk, 0.0).astype(BF16))
        y = _dot(x, t_scr[:, cols]) + _dot_nt(sp, nt_scr[cols, :])
        for hp in range(p0, p0 + SSM_PANEL):
            y_ref[pl.ds(hp, rows, stride=hh), :] = y[:, (hp - p0) * cc:(hp - p0 + 1) * cc]


def _ssm(u3, pw, bc, dtap, a_pow, bsz):
    hh = SSM_GROUP_SIZE
    _, rows16, cc = u3.shape
    n_chunks = rows16 // hh // bsz
    width = hh * cc
    half = 2 * SSM_STATE
    blk = pl.BlockSpec((None, rows16, cc), lambda g: (g, 0, 0))
    per_group = lambda a: pl.BlockSpec((None,) + a.shape[1:], lambda g: (g,) + (0,) * (a.ndim - 1))
    return pl.pallas_call(
        functools.partial(_ssm_body, bsz=bsz),
        grid=(SSM_GROUPS,),
        in_specs=[blk, per_group(pw), per_group(bc), per_group(dtap), per_group(a_pow)],
        out_specs=blk,
        out_shape=jax.ShapeDtypeStruct(u3.shape, F32),
        scratch_shapes=[
            pltpu.VMEM((hh * hh, cc), F32),
            pltpu.VMEM((width, width), BF16),
            pltpu.VMEM((width, 2 * half), BF16),
            pltpu.VMEM((width, 2 * half), BF16),
            pltpu.VMEM((bsz, n_chunks, 2 * half), F32),
            pltpu.VMEM((bsz, n_chunks, 2 * half), F32),
        ],
        compiler_params=_params(1),
        name="ssm",
    )(u3, pw, bc, dtap, a_pow)


def _ssm_operators(lam_re, lam_im, log_dt, b_re, b_im, c_re, c_im, d_skip):
    cc, pp, hh, gg = SSM_CHUNK, SSM_STATE, SSM_GROUP_SIZE, SSM_GROUPS
    lr, li = lam_re.astype(F32), lam_im.astype(F32)
    dt = jnp.exp(log_dt.astype(F32))[:, None]
    mag = jnp.exp(lr * dt)
    a_re, a_im = mag * jnp.cos(li * dt), mag * jnp.sin(li * dt)
    nr, ni = a_re - 1.0, a_im
    den = lr * lr + li * li
    coef_re = (nr * lr + ni * li) / den
    coef_im = (ni * lr - nr * li) / den
    br_, bi_ = b_re.astype(F32), b_im.astype(F32)
    bb_re = coef_re[..., None] * br_ - coef_im[..., None] * bi_
    bb_im = coef_re[..., None] * bi_ + coef_im[..., None] * br_
    cr, ci = c_re.astype(F32), c_im.astype(F32)
    dd = d_skip.astype(F32)

    lane_pad = lambda v: jnp.pad(v, [(0, 0)] * (v.ndim - 1) + [(0, pp)])
    log_re, log_im = lane_pad(lr * dt), lane_pad(li * dt)
    live = (jnp.arange(2 * pp) < pp).astype(F32)

    def powers(kvec, state_last):
        ix = (None, slice(None), None) if state_last else (None, None, slice(None))
        sx = (slice(None), None, slice(None)) if state_last else (slice(None), slice(None), None)
        mag = jnp.exp(kvec[ix] * log_re[sx]) * live[sx[1:]][None]
        ang = kvec[ix] * log_im[sx]
        return mag * jnp.cos(ang), mag * jnp.sin(ang)

    ks = jnp.arange(cc, dtype=F32)
    pw = jnp.stack(powers(cc - 1 - ks, True) + powers(ks + 1, True) + powers(ks, False),
                   axis=1)
    bc = jnp.stack([lane_pad(bb_re.transpose(0, 2, 1)), lane_pad(bb_im.transpose(0, 2, 1)),
                    lane_pad(cr), lane_pad(ci)], axis=1)
    pair = jnp.arange(hh * hh)
    on_diag = jnp.logical_and((pair // hh == pair % hh)[:, None], (jnp.arange(cc) == 0)[None, :])
    dtap = jnp.repeat(dd, hh, axis=1)[:, :, None] * on_diag.astype(F32)[None]
    ac_re, ac_im = powers(jnp.full((1,), cc, F32), True)
    a_pow = jnp.concatenate([ac_re, ac_im, jnp.zeros((gg, 6, 2 * pp), F32)], axis=1)
    return pw, bc, dtap, a_pow


def _attn_tile(batches, pos_of, s_scr, p_scr):
    w, e, nh = ATTN_BLOCK, ATTN_HEAD_DIM, ATTN_HEADS_PER_GROUP
    lane = lax.broadcasted_iota(jnp.int32, (w, 2 * w), 1)
    pkey = pos_of(lane & (w - 1))
    pqry = pos_of(lax.broadcasted_iota(jnp.int32, (w, 2 * w), 0))
    mask = jnp.logical_or(jnp.logical_and(lane < w, pkey >= pqry),
                          jnp.logical_and(lane >= w, pkey <= pqry))
    head_of_lane = lane // e
    scale = e ** -0.5
    slot0 = [0]
    for blocks in batches:
        slot0.append(slot0[-1] + len(blocks) * nh)

    def scores(k):
        for bi, (q, kcat, _, has_prev, _) in enumerate(batches[k]):
            qs = q * scale
            qm = jnp.concatenate([jnp.where(head_of_lane == h, qs, 0) for h in range(nh)], axis=0)
            s = _dot_nt(qm, kcat).reshape(nh, w, 2 * w)
            if has_prev is not None:
                s = jnp.where(jnp.logical_or(lane >= w, has_prev)[None], s, NEG_INF)
            s_scr[slot0[k] + bi * nh:slot0[k] + (bi + 1) * nh] = s

    def softmax(k):
        s_all = jnp.where(mask[None], s_scr[slot0[k]:slot0[k + 1]], NEG_INF)
        mx = jnp.max(s_all, axis=-1, keepdims=True)
        p = jnp.exp(s_all - mx)
        den = jnp.sum(p, axis=-1, keepdims=True)
        p_scr[slot0[k]:slot0[k + 1]] = p.astype(BF16)
        return mx, den

    def values(k, mx, den):
        for bi, (_, _, vcat, _, write) in enumerate(batches[k]):
            lo = slot0[k] + bi * nh
            pv = _dot(p_scr[lo:lo + nh].reshape(nh * w, 2 * w), vcat).reshape(nh, w, nh * e)
            wide = lambda st: jnp.concatenate(
                [jnp.broadcast_to(st[bi * nh + h], (w, e)) for h in range(nh)], axis=1)
            den_t = wide(den)
            num = jnp.concatenate([pv[h][:, h * e:(h + 1) * e] for h in range(nh)], axis=1)
            write(num * (1.0 / den_t), wide(mx) + jnp.log(den_t))

    scores(0)
    for k in range(len(batches)):
        if k + 1 < len(batches):
            scores(k + 1)
        values(k, *softmax(k))


def _dil_attn_nat_body(q_ref, kc_ref, kh_ref, vc_ref, vh_ref, o_ref, l_ref, s_scr, p_scr):
    i = pl.program_id(1)
    w = ATTN_BLOCK
    pos_of = lambda idx: idx
    n_blk = q_ref.shape[0] // w
    batches = []
    for j0 in range(0, n_blk, ATTN_BATCH):
        blocks = []
        for j in range(j0, j0 + ATTN_BATCH):
            rs = slice(j * w, (j + 1) * w)
            if j == 0:
                kcat = jnp.concatenate([kh_ref[...], kc_ref[rs, :]], axis=0)
                vcat = jnp.concatenate([vh_ref[...], vc_ref[rs, :]], axis=0)
                has_prev = i > 0
            else:
                kcat, vcat, has_prev = kc_ref[(j - 1) * w:(j + 1) * w, :], vc_ref[(j - 1) * w:(j + 1) * w, :], None

            def write(o, lse, rs=rs):
                for half in range(ATTN_GROUP_WIDTH // LANES):
                    ls = slice(half * LANES, (half + 1) * LANES)
                    o_ref[half, rs, :] = o[:, ls]
                    l_ref[half, rs, :] = lse[:, ls]

            blocks.append((q_ref[rs, :], kcat, vcat, has_prev, write))
        batches.append(blocks)
    _attn_tile(batches, pos_of, s_scr, p_scr)


def _dil_attn_res_body(q_ref, kc_ref, kh_ref, vc_ref, vh_ref, o_ref, l_ref, s_scr, p_scr, *, dilation):
    i = pl.program_id(1)
    w = ATTN_BLOCK
    nq = MAX_DILATION // dilation
    rr = w // nq
    shift = rr.bit_length() - 1
    pos_of = lambda idx: nq * (idx & (rr - 1)) + (idx >> shift)

    def slabs(ref, r, lo):
        return [ref[q * dilation + r, lo:lo + rr, :] for q in range(nq)]

    todo = [(r, b) for r in range(dilation) for b in range(nq)]
    batches = []
    for j0 in range(0, len(todo), ATTN_BATCH):
        blocks = []
        for r, b in todo[j0:j0 + ATTN_BATCH]:
            if b == 0:
                kprev = [kh_ref[q * dilation + r] for q in range(nq)]
                vprev = [vh_ref[q * dilation + r] for q in range(nq)]
                has_prev = i > 0
            else:
                kprev, vprev, has_prev = slabs(kc_ref, r, (b - 1) * rr), slabs(vc_ref, r, (b - 1) * rr), None
            kcat = jnp.concatenate(kprev + slabs(kc_ref, r, b * rr), axis=0)
            vcat = jnp.concatenate(vprev + slabs(vc_ref, r, b * rr), axis=0)
            q = jnp.concatenate(slabs(q_ref, r, b * rr), axis=0)

            def write(o, lse, r=r, b=b):
                for qq in range(nq):
                    dst = pl.ds(MAX_DILATION * b * rr + qq * dilation + r, rr, stride=MAX_DILATION)
                    for half in range(ATTN_GROUP_WIDTH // LANES):
                        ls = slice(half * LANES, (half + 1) * LANES)
                        o_ref[half, dst, :] = o[qq * rr:(qq + 1) * rr, ls]
                        l_ref[half, dst, :] = lse[qq * rr:(qq + 1) * rr, ls]

            blocks.append((q, kcat, vcat, has_prev, write))
        batches.append(blocks)
    _attn_tile(batches, pos_of, s_scr, p_scr)


def _dil_attn(qkv, col0, dilation, bsz, seq):
    gw, w = ATTN_GROUP_WIDTH, ATTN_BLOCK
    nt = seq // ATTN_TILE
    n_sc = ATTN_TILE // w * ATTN_HEADS_PER_GROUP
    scratch = [pltpu.VMEM((n_sc, w, 2 * w), F32), pltpu.VMEM((n_sc, w, 2 * w), BF16)]
    out_spec = pl.BlockSpec((gw // LANES, ATTN_TILE, LANES), lambda b, i: (0, b * nt + i, 0))
    out_shape = [jax.ShapeDtypeStruct((gw // LANES, bsz * seq, LANES), F32)] * 2
    if dilation == 1:
        bpt = ATTN_TILE // w
        cur = lambda c: pl.BlockSpec((ATTN_TILE, gw), lambda b, i: (b * nt + i, c))
        halo = lambda c: pl.BlockSpec((w, gw), lambda b, i: (jnp.maximum((b * nt + i) * bpt - 1, 0), c))
        body = _dil_attn_nat_body
    else:
        rr = w // (MAX_DILATION // dilation)
        bpt = w // rr
        cur = lambda c: pl.BlockSpec((None, MAX_DILATION, w, gw), lambda b, i: (b, 0, i, c))
        halo = lambda c: pl.BlockSpec((None, MAX_DILATION, rr, gw),
                                      lambda b, i: (b, 0, jnp.maximum(i * bpt - 1, 0), c))
        body = functools.partial(_dil_attn_res_body, dilation=dilation)
    return pl.pallas_call(
        body,
        grid=(bsz, nt),
        in_specs=[cur(col0), cur(col0 + 1), halo(col0 + 1), cur(col0 + 2), halo(col0 + 2)],
        out_specs=[out_spec, out_spec],
        out_shape=out_shape,
        scratch_shapes=scratch,
        compiler_params=_params(2),
        name=f"dil_attn_d{dilation}",
    )(qkv, qkv, qkv, qkv, qkv)


def _mem_kv_body(mem_ref, g_ref, w_ref, k_ref, v_ref):
    n = _rms(mem_ref[...], g_ref[...]).astype(BF16)
    kv = _dot(n, w_ref[...])
    k_ref[...] = kv[:, :MEM_WIDTH].astype(BF16)
    v_ref[...] = kv[:, MEM_WIDTH:].astype(BF16)


def _mem_kv(mem, g, w_kv):
    bsz, mem_len, _ = mem.shape
    blk = pl.BlockSpec((None, mem_len, MEM_WIDTH), lambda b: (b, 0, 0))
    return pl.pallas_call(
        _mem_kv_body,
        grid=(bsz,),
        in_specs=[pl.BlockSpec((None, mem_len, D_MODEL), lambda b: (b, 0, 0)),
                  _resident((1, D_MODEL)), _resident(w_kv.shape)],
        out_specs=[blk, blk],
        out_shape=[jax.ShapeDtypeStruct((bsz, mem_len, MEM_WIDTH), BF16)] * 2,
        compiler_params=_params(1),
        name="mem_kv",
    )(mem, g, w_kv)


def _merge_body(y_ref, o0_ref, o1_ref, o2_ref, l0_ref, l1_ref, l2_ref, mk_ref, mv_ref, x_ref,
                g1_ref, wmq_ref, wg_ref, wglut_ref, bglu_ref, wssm_ref, wattn_ref, wmem_ref, wo_ref,
                bgate_ref, h_ref):
    n = _rms(x_ref[...], g1_ref[...]).astype(BF16)

    hh = SSM_GROUP_SIZE
    yt = jnp.concatenate([y_ref[:, c * hh:(c + 1) * hh, :].reshape(SSM_WIDTH, SSM_CHUNK)
                          for c in range(y_ref.shape[1] // hh)], axis=1)
    gy = _twice_gelu_tanh(yt)
    glu = _dot(wglut_ref[...], gy.astype(BF16)) + bglu_ref[...]
    y2 = (gy * _twice_sigmoid_of_twice(glu)).T.astype(BF16)
    br_ssm = _dot(y2, wssm_ref[...])

    halves = []
    for half in range(ATTN_GROUP_WIDTH // LANES):
        l0, l1, l2 = l0_ref[half], l1_ref[half], l2_ref[half]
        mx = jnp.maximum(jnp.maximum(l0, l1), l2)
        e0, e1, e2 = jnp.exp(l0 - mx), jnp.exp(l1 - mx), jnp.exp(l2 - mx)
        halves.append((e0 * o0_ref[half] + e1 * o1_ref[half] + e2 * o2_ref[half]) / (e0 + e1 + e2))
    br_attn = _dot(jnp.concatenate(halves, axis=1).astype(BF16), wattn_ref[...])

    e = MEM_HEAD_DIM
    scale = e ** -0.5
    mq = _dot(n, wmq_ref[...]).astype(BF16)
    heads = []
    for h in range(MEM_HEADS):
        sl = slice(h * e, (h + 1) * e)
        s = _dot_nt(mq[:, sl], mk_ref[:, sl]) * scale
        p = jnp.exp(s - jnp.max(s, axis=-1, keepdims=True))
        inv = 1.0 / jnp.sum(p, axis=-1, keepdims=True)
        heads.append(_dot(p.astype(BF16), mv_ref[:, sl]) * inv)
    br_mem = _dot(jnp.concatenate(heads, axis=1).astype(BF16), wmem_ref[...])

    d = D_MODEL
    gate = lambda k: _twice_sigmoid_of_twice(
        _dot(n, wg_ref[:, k * d:(k + 1) * d]) + bgate_ref[:, k * d:(k + 1) * d])
    merged = gate(0) * br_ssm + gate(1) * br_attn + gate(2) * br_mem
    h_ref[...] = x_ref[...] + _dot(merged.astype(BF16), wo_ref[...])


def _merge(y3, os, ls, mk, mv, x2, weights, bsz, seq, tm):
    nt = seq // tm
    cpt = tm // SSM_CHUNK
    mem_len = mk.shape[1]
    row = lambda b, i: (b * nt + i, 0)
    tok = lambda width: pl.BlockSpec((tm, width), row)
    per_batch = pl.BlockSpec((None, mem_len, MEM_WIDTH), lambda b, i: (b, 0, 0))
    return pl.pallas_call(
        _merge_body,
        grid=(bsz, nt),
        in_specs=[pl.BlockSpec((SSM_GROUPS, cpt * SSM_GROUP_SIZE, SSM_CHUNK), lambda b, i: (0, b * nt + i, 0))]
        + [pl.BlockSpec((ATTN_GROUP_WIDTH // LANES, tm, LANES), lambda b, i: (0, b * nt + i, 0))] * 6
        + [per_batch, per_batch, tok(D_MODEL)]
        + [_resident(w.shape) for w in weights],
        out_specs=tok(D_MODEL),
        out_shape=jax.ShapeDtypeStruct((bsz * seq, D_MODEL), F32),
        compiler_params=_params(2),
        name="merge",
    )(y3, *os, *ls, mk, mv, x2, *weights)


def _mlp_body(h_ref, g2_ref, wup_ref, wdown_ref, gf_ref, out_ref, *, final_norm, ff_chunk):
    h = h_ref[...]
    n2 = _rms(h, g2_ref[...]).astype(BF16)
    acc = h
    for c in range(0, D_FF, ff_chunk):
        a = jnp.maximum(_dot(n2, wup_ref[:, c:c + ff_chunk]), 0.0)
        acc = acc + _dot((a * a).astype(BF16), wdown_ref[c:c + ff_chunk, :])
    out_ref[...] = _rms(acc, gf_ref[...]) if final_norm else acc


def _mlp(h, g2, w_up, w_down, gf, final_norm, tm):
    tokens = h.shape[0]
    row = lambda i: (i, 0)
    return pl.pallas_call(
        functools.partial(_mlp_body, final_norm=final_norm, ff_chunk=D_MODEL),
        grid=(tokens // tm,),
        in_specs=[pl.BlockSpec((tm, D_MODEL), row), _resident((1, D_MODEL)),
                  _resident(w_up.shape), _resident(w_down.shape), _resident((1, D_MODEL))],
        out_specs=pl.BlockSpec((tm, D_MODEL), row),
        out_shape=jax.ShapeDtypeStruct((tokens, D_MODEL), F32),
        compiler_params=_params(1),
        name="mlp",
    )(h, g2, w_up, w_down, gf)


def kernel(x, mem, norm1_g, mem_norm_g, w_in, b_gate, ssm_lambda_re, ssm_lambda_im, ssm_log_dt, ssm_b_re, ssm_b_im, ssm_c_re, ssm_c_im, ssm_d, w_glu, b_glu, w_ssm_br, w_attn_br, w_mem_kv, w_mem_br, w_o, norm2_g, w_up, w_down, final_g):
    bsz, seq, d_model = x.shape
    depth = w_in.shape[0]
    assert d_model == D_MODEL and seq % ATTN_TILE == 0
    assert tuple(w // d for w, d in ATTN_PATTERNS) == (ATTN_BLOCK,) * 3
    assert tuple(d for _, d in ATTN_PATTERNS) == (1, 4, 16)
    tm = 512
    h = x.reshape(bsz * seq, D_MODEL)
    row = lambda v: v.reshape(1, -1).astype(F32)
    aw, gw = ATTN_WIDTH, ATTN_GROUP_WIDTH
    o_q, o_k, o_v = SSM_WIDTH, SSM_WIDTH + aw, SSM_WIDTH + 2 * aw
    o_mq = SSM_WIDTH + 3 * aw
    o_g = o_mq + MEM_WIDTH
    for i in range(depth):
        wi = w_in[i]
        cols = lambda g: [wi[:, o + g * gw:o + (g + 1) * gw] for o in (o_q, o_k, o_v)]
        w_qkv0 = jnp.concatenate(cols(0), axis=1).astype(BF16)
        w_qkvd = jnp.concatenate(cols(1) + cols(2), axis=1).astype(BF16)
        u3, qkv0, qkvd = _in_proj(
            h, row(norm1_g[i]), wi[:, :SSM_WIDTH].T.astype(BF16), w_qkv0, w_qkvd, bsz, seq, 2 * tm)

        ops = _ssm_operators(ssm_lambda_re[i], ssm_lambda_im[i], ssm_log_dt[i], ssm_b_re[i],
                             ssm_b_im[i], ssm_c_re[i], ssm_c_im[i], ssm_d[i])
        y3 = _ssm(u3, *ops, bsz)

        outs = [_dil_attn(qkv0, 0, 1, bsz, seq), _dil_attn(qkvd, 0, 4, bsz, seq),
                _dil_attn(qkvd, 3, 16, bsz, seq)]
        os, ls = [o for o, _ in outs], [l for _, l in outs]

        mk, mv = _mem_kv(mem, row(mem_norm_g[i]), w_mem_kv[i].astype(BF16))

        merge_weights = [
            row(norm1_g[i]), wi[:, o_mq:o_g].astype(BF16), (0.5 * wi[:, o_g:]).astype(BF16),
            (0.25 * w_glu[i]).T.astype(BF16), (0.5 * b_glu[i]).reshape(-1, 1).astype(F32),
            (0.125 * w_ssm_br[i]).astype(BF16), (0.5 * w_attn_br[i]).astype(BF16),
            (0.5 * w_mem_br[i]).astype(BF16), w_o[i].astype(BF16), row(0.5 * b_gate[i])]
        h = _merge(y3, os, ls, mk, mv, h, merge_weights, bsz, seq, tm)

        h = _mlp(h, row(norm2_g[i]), w_up[i].astype(BF16), w_down[i].astype(BF16), row(final_g),
                 final_norm=(i == depth - 1), tm=2 * tm)
    return h.reshape(bsz, seq, D_MODEL)
```

```python
import functools
import math

import jax
import jax.numpy as jnp
from jax import lax
from jax.experimental import pallas as pl
from jax.experimental.pallas import tpu as pltpu

D_MODEL = 1024
SSM_GROUP_SIZE = 16
SSM_GROUPS = 32
SSM_WIDTH = SSM_GROUP_SIZE * SSM_GROUPS
SSM_STATE = 64
SSM_CHUNK = 128
SSM_PANEL = 2
ATTN_PATTERNS = ((128, 1), (512, 4), (2048, 16))
ATTN_HEADS_PER_GROUP = 4
ATTN_HEAD_DIM = 64
ATTN_GROUP_WIDTH = ATTN_HEADS_PER_GROUP * ATTN_HEAD_DIM
ATTN_WIDTH = ATTN_GROUP_WIDTH * len(ATTN_PATTERNS)
ATTN_BLOCK = 128
MAX_DILATION = 16
ATTN_TILE = ATTN_BLOCK * MAX_DILATION
ATTN_BATCH = 1
MEM_HEADS = 4
MEM_HEAD_DIM = 128
MEM_WIDTH = MEM_HEADS * MEM_HEAD_DIM
N_BRANCHES = 3
D_FF = 4 * D_MODEL
RMS_EPS = 1e-6
NEG_INF = -1e30

LANES = 128

VMEM_LIMIT_BYTES = 56 * 1024 * 1024

F32 = jnp.float32
BF16 = jnp.bfloat16


def _params(n_axes):
    return pltpu.CompilerParams(
        dimension_semantics=("arbitrary",) * n_axes, vmem_limit_bytes=VMEM_LIMIT_BYTES)


def _resident(shape):
    nd = len(shape)
    return pl.BlockSpec(shape, lambda *_: (0,) * nd, pipeline_mode=pl.Buffered(1))


def _rms(x, g):
    ms = jnp.mean(x * x, axis=-1, keepdims=True)
    return x * lax.rsqrt(ms + RMS_EPS) * g


def _twice_sigmoid_of_twice(x):
    return 1.0 + jnp.tanh(x)


def _twice_gelu_tanh(x):
    c = math.sqrt(2.0 / math.pi)
    return x * (1.0 + jnp.tanh(x * (c + (c * 0.044715) * (x * x))))


def _dot(a, b):
    return jnp.dot(a, b, preferred_element_type=F32)


def _dot_nt(a, b):
    return lax.dot_general(a, b, (((1,), (1,)), ((), ())), preferred_element_type=F32)


def _in_proj_body(x_ref, g_ref, wut_ref, wqkv0_ref, wqkvd_ref, u_ref, qkv0_ref, qkvd_ref, n_scr):
    tm = x_ref.shape[0]
    nf = _rms(x_ref[...], g_ref[...])
    n = nf.astype(BF16)
    for c in range(D_MODEL // LANES):
        n_scr[c] = nf[:, c * LANES:(c + 1) * LANES]
    ut = _dot_nt(wut_ref[...], n)
    for c in range(tm // SSM_CHUNK):
        u_ref[:, c * SSM_GROUP_SIZE:(c + 1) * SSM_GROUP_SIZE, :] = (
            ut[:, c * SSM_CHUNK:(c + 1) * SSM_CHUNK].reshape(SSM_GROUPS, SSM_GROUP_SIZE, SSM_CHUNK))
    qkv0_ref[...] = _dot(n, wqkv0_ref[...]).astype(BF16)
    rows = tm // MAX_DILATION
    n_perm = jnp.concatenate(
        [jnp.concatenate([n_scr[c, pl.ds(r, rows, stride=MAX_DILATION), :]
                          for c in range(D_MODEL // LANES)], axis=1)
         for r in range(MAX_DILATION)], axis=0)
    zp = _dot(n_perm.astype(BF16), wqkvd_ref[...]).astype(BF16)
    for r in range(MAX_DILATION):
        qkvd_ref[r] = zp[r * rows:(r + 1) * rows]


def _in_proj(x2, g, w_ut, w_qkv0, w_qkvd, bsz, seq, tm):
    tokens = bsz * seq
    nt = seq // tm
    cpt = tm // SSM_CHUNK
    n_chunks = seq // SSM_CHUNK
    wd = w_qkvd.shape[1]
    row = lambda b, i: (b * nt + i, 0)
    return pl.pallas_call(
        _in_proj_body,
        grid=(bsz, nt),
        in_specs=[
            pl.BlockSpec((tm, D_MODEL), row),
            _resident((1, D_MODEL)),
            _resident(w_ut.shape), _resident(w_qkv0.shape), _resident(w_qkvd.shape),
        ],
        out_specs=[
            pl.BlockSpec((SSM_GROUPS, cpt * SSM_GROUP_SIZE, SSM_CHUNK), lambda b, i: (0, b * nt + i, 0)),
            pl.BlockSpec((tm, ATTN_WIDTH), row),
            pl.BlockSpec((None, MAX_DILATION, tm // MAX_DILATION, wd), lambda b, i: (b, 0, i, 0)),
        ],
        out_shape=[
            jax.ShapeDtypeStruct((SSM_GROUPS, bsz * n_chunks * SSM_GROUP_SIZE, SSM_CHUNK), F32),
            jax.ShapeDtypeStruct((tokens, ATTN_WIDTH), BF16),
            jax.ShapeDtypeStruct((bsz, MAX_DILATION, seq // MAX_DILATION, wd), BF16),
        ],
        scratch_shapes=[pltpu.VMEM((D_MODEL // LANES, tm, LANES), F32)],
        compiler_params=_params(2),
        name="in_proj",
    )(x2, g, w_ut, w_qkv0, w_qkvd)


def _ssm_body(u_ref, pk_ref, pq_ref, bc_ref, dtap_ref, y_ref,
              k_scr, t_scr, m_scr, nt_scr, z_scr, sp_scr, *, bsz):
    hh = SSM_GROUP_SIZE
    cc = u_ref.shape[1]
    rows = u_ref.shape[0] // hh
    n_chunks = rows // bsz
    half = 2 * SSM_STATE

    q_re, q_im = pq_ref[0], pq_ref[1]
    r_re, r_im = pk_ref[0, 1:cc + 1, :], pk_ref[1, 1:cc + 1, :]
    b_re, b_im, c_re, c_im = bc_ref[0], bc_ref[1], bc_ref[2], bc_ref[3]
    w_re, w_im = [], []
    for h in range(hh):
        rs = slice(h * cc, (h + 1) * cc)
        bre, bim, cre, cim = b_re[h:h + 1], b_im[h:h + 1], c_re[h:h + 1], c_im[h:h + 1]
        m_scr[rs, 0:half] = (q_re * bre - q_im * bim).astype(BF16)
        m_scr[rs, half:2 * half] = (q_re * bim + q_im * bre).astype(BF16)
        nt_scr[rs, 0:half] = (r_re * cre - r_im * cim).astype(BF16)
        nt_scr[rs, half:2 * half] = (-(r_im * cre + r_re * cim)).astype(BF16)
        w_re.append(c_re * bre - c_im * bim)
        w_im.append(c_re * bim + c_im * bre)
    taps_dot = lambda w, p: lax.dot_general(
        w, p, (((1,), (1,)), ((), ())), precision=lax.Precision.HIGHEST, preferred_element_type=F32)
    k_scr[...] = (taps_dot(jnp.concatenate(w_re, axis=0), pk_ref[0, 0:cc, :])
                  - taps_dot(jnp.concatenate(w_im, axis=0), pk_ref[1, 0:cc, :])
                  + dtap_ref[...])

    srow = lax.broadcasted_iota(jnp.int32, (cc, cc), 0)
    tcol = lax.broadcasted_iota(jnp.int32, (cc, cc), 1)
    causal = tcol >= srow

    x = jnp.concatenate([u_ref[pl.ds(h, rows, stride=hh), :].astype(BF16) for h in range(hh)],
                        axis=1)
    z_scr[...] = _dot(x, m_scr[...]).reshape(bsz, n_chunks, 2 * half)
    ar = pk_ref[0, cc:cc + 1, :].reshape(1, 1, half)
    ai = pk_ref[1, cc:cc + 1, :].reshape(1, 1, half)
    sr = jnp.zeros((bsz, 1, half), F32)
    si = jnp.zeros((bsz, 1, half), F32)
    for c in range(n_chunks):
        sp_scr[:, c:c + 1, 0:half] = sr
        sp_scr[:, c:c + 1, half:2 * half] = si
        zr = z_scr[:, c:c + 1, 0:half]
        zi = z_scr[:, c:c + 1, half:2 * half]
        sr, si = ar * sr - ai * si + zr, ar * si + ai * sr + zi
    sp = sp_scr[...].reshape(rows, 2 * half).astype(BF16)

    for p0 in range(0, hh, SSM_PANEL):
        cols = slice(p0 * cc, (p0 + SSM_PANEL) * cc)
        for h in range(hh):
            for hp in range(p0, p0 + SSM_PANEL):
                kb = jnp.broadcast_to(k_scr[h * hh + hp:h * hh + hp + 1, :], (cc, cc))
                blk = pltpu.roll(kb, 0, 1, stride=1, stride_axis=0)
                t_scr[h * cc:(h + 1) * cc, hp * cc:(hp + 1) * cc] = (
                    jnp.where(causal, blk, 0.0).astype(BF16))
        y = _dot(x, t_scr[:, cols]) + _dot_nt(sp, nt_scr[cols, :])
        for hp in range(p0, p0 + SSM_PANEL):
            y_ref[pl.ds(hp, rows, stride=hh), :] = y[:, (hp - p0) * cc:(hp - p0 + 1) * cc]


def _ssm(u3, pk, pq, bc, dtap, bsz):
    hh = SSM_GROUP_SIZE
    _, rows16, cc = u3.shape
    n_chunks = rows16 // hh // bsz
    width = hh * cc
    half = 2 * SSM_STATE
    blk = pl.BlockSpec((None, rows16, cc), lambda g: (g, 0, 0))
    per_group = lambda a: pl.BlockSpec((None,) + a.shape[1:], lambda g: (g,) + (0,) * (a.ndim - 1))
    return pl.pallas_call(
        functools.partial(_ssm_body, bsz=bsz),
        grid=(SSM_GROUPS,),
        in_specs=[blk, per_group(pk), per_group(pq), per_group(bc), per_group(dtap)],
        out_specs=blk,
        out_shape=jax.ShapeDtypeStruct(u3.shape, F32),
        scratch_shapes=[
            pltpu.VMEM((hh * hh, cc), F32),
            pltpu.VMEM((width, width), BF16),
            pltpu.VMEM((width, 2 * half), BF16),
            pltpu.VMEM((width, 2 * half), BF16),
            pltpu.VMEM((bsz, n_chunks, 2 * half), F32),
            pltpu.VMEM((bsz, n_chunks, 2 * half), F32),
        ],
        compiler_params=_params(1),
        name="ssm",
    )(u3, pk, pq, bc, dtap)


def _ssm_operators(lam_re, lam_im, log_dt, b_re, b_im, c_re, c_im, d_skip):
    cc, pp, hh, gg = SSM_CHUNK, SSM_STATE, SSM_GROUP_SIZE, SSM_GROUPS
    lr, li = lam_re.astype(F32), lam_im.astype(F32)
    dt = jnp.exp(log_dt.astype(F32))[:, None]
    mag = jnp.exp(lr * dt)
    a_re, a_im = mag * jnp.cos(li * dt), mag * jnp.sin(li * dt)
    nr, ni = a_re - 1.0, a_im
    den = lr * lr + li * li
    coef_re = (nr * lr + ni * li) / den
    coef_im = (ni * lr - nr * li) / den
    br_, bi_ = b_re.astype(F32), b_im.astype(F32)
    bb_re = coef_re[..., None] * br_ - coef_im[..., None] * bi_
    bb_im = coef_re[..., None] * bi_ + coef_im[..., None] * br_
    cr, ci = c_re.astype(F32), c_im.astype(F32)
    dd = d_skip.astype(F32)

    lane_pad = lambda v: jnp.pad(v, [(0, 0)] * (v.ndim - 1) + [(0, pp)])
    log_re, log_im = lane_pad(lr * dt), lane_pad(li * dt)
    live = (jnp.arange(2 * pp) < pp).astype(F32)
    ks = jnp.arange(cc + 8, dtype=F32)[None, :, None]
    mag = jnp.exp(ks * log_re[:, None, :]) * live[None, None, :]
    ang = ks * log_im[:, None, :]
    pk = jnp.stack([mag * jnp.cos(ang), mag * jnp.sin(ang)], axis=1)
    pq = pk[:, :, cc - 1::-1, :]
    bc = jnp.stack([lane_pad(bb_re.transpose(0, 2, 1)), lane_pad(bb_im.transpose(0, 2, 1)),
                    lane_pad(cr), lane_pad(ci)], axis=1)
    pair = jnp.arange(hh * hh)
    on_diag = jnp.logical_and((pair // hh == pair % hh)[:, None], (jnp.arange(cc) == 0)[None, :])
    dtap = jnp.repeat(dd, hh, axis=1)[:, :, None] * on_diag.astype(F32)[None]
    return pk, pq, bc, dtap


def _attn_tile(batches, pos_of, s_scr, p_scr):
    w, e, nh = ATTN_BLOCK, ATTN_HEAD_DIM, ATTN_HEADS_PER_GROUP
    lane = lax.broadcasted_iota(jnp.int32, (w, 2 * w), 1)
    pkey = pos_of(lane & (w - 1))
    pqry = pos_of(lax.broadcasted_iota(jnp.int32, (w, 2 * w), 0))
    mask = jnp.logical_or(jnp.logical_and(lane < w, pkey >= pqry),
                          jnp.logical_and(lane >= w, pkey <= pqry))
    head_of_lane = lane // e
    scale = e ** -0.5
    slot0 = [0]
    for blocks in batches:
        slot0.append(slot0[-1] + len(blocks) * nh)

    def scores(k):
        for bi, (q, kcat, _, has_prev, _) in enumerate(batches[k]):
            qs = q * scale
            qm = jnp.concatenate([jnp.where(head_of_lane == h, qs, 0) for h in range(nh)], axis=0)
            s = _dot_nt(qm, kcat).reshape(nh, w, 2 * w)
            if has_prev is not None:
                s = jnp.where(jnp.logical_or(lane >= w, has_prev)[None], s, NEG_INF)
            s_scr[slot0[k] + bi * nh:slot0[k] + (bi + 1) * nh] = s

    def softmax(k):
        s_all = jnp.where(mask[None], s_scr[slot0[k]:slot0[k + 1]], NEG_INF)
        mx = jnp.max(s_all, axis=-1, keepdims=True)
        p = jnp.exp(s_all - mx)
        den = jnp.sum(p, axis=-1, keepdims=True)
        p_scr[slot0[k]:slot0[k + 1]] = p.astype(BF16)
        return mx, den

    def values(k, mx, den):
        for bi, (_, _, vcat, _, write) in enumerate(batches[k]):
            lo = slot0[k] + bi * nh
            pv = _dot(p_scr[lo:lo + nh].reshape(nh * w, 2 * w), vcat).reshape(nh, w, nh * e)
            wide = lambda st: jnp.concatenate(
                [jnp.broadcast_to(st[bi * nh + h], (w, e)) for h in range(nh)], axis=1)
            den_t = wide(den)
            num = jnp.concatenate([pv[h][:, h * e:(h + 1) * e] for h in range(nh)], axis=1)
            write(num * (1.0 / den_t), wide(mx) + jnp.log(den_t))

    scores(0)
    for k in range(len(batches)):
        if k + 1 < len(batches):
            scores(k + 1)
        values(k, *softmax(k))


def _dil_attn_nat_body(q_ref, kc_ref, kh_ref, vc_ref, vh_ref, o_ref, l_ref, s_scr, p_scr):
    i = pl.program_id(1)
    w = ATTN_BLOCK
    pos_of = lambda idx: idx
    n_blk = q_ref.shape[0] // w
    batches = []
    for j0 in range(0, n_blk, ATTN_BATCH):
        blocks = []
        for j in range(j0, j0 + ATTN_BATCH):
            rs = slice(j * w, (j + 1) * w)
            if j == 0:
                kcat = jnp.concatenate([kh_ref[...], kc_ref[rs, :]], axis=0)
                vcat = jnp.concatenate([vh_ref[...], vc_ref[rs, :]], axis=0)
                has_prev = i > 0
            else:
                kcat, vcat, has_prev = kc_ref[(j - 1) * w:(j + 1) * w, :], vc_ref[(j - 1) * w:(j + 1) * w, :], None

            def write(o, lse, rs=rs):
                for half in range(ATTN_GROUP_WIDTH // LANES):
                    ls = slice(half * LANES, (half + 1) * LANES)
                    o_ref[half, rs, :] = o[:, ls]
                    l_ref[half, rs, :] = lse[:, ls]

            blocks.append((q_ref[rs, :], kcat, vcat, has_prev, write))
        batches.append(blocks)
    _attn_tile(batches, pos_of, s_scr, p_scr)


def _dil_attn_res_body(q_ref, kc_ref, kh_ref, vc_ref, vh_ref, o_ref, l_ref, s_scr, p_scr, *, dilation):
    i = pl.program_id(1)
    w = ATTN_BLOCK
    nq = MAX_DILATION // dilation
    rr = w // nq
    shift = rr.bit_length() - 1
    pos_of = lambda idx: nq * (idx & (rr - 1)) + (idx >> shift)

    def slabs(ref, r, lo):
        return [ref[q * dilation + r, lo:lo + rr, :] for q in range(nq)]

    todo = [(r, b) for r in range(dilation) for b in range(nq)]
    batches = []
    for j0 in range(0, len(todo), ATTN_BATCH):
        blocks = []
        for r, b in todo[j0:j0 + ATTN_BATCH]:
            if b == 0:
                kprev = [kh_ref[q * dilation + r] for q in range(nq)]
                vprev = [vh_ref[q * dilation + r] for q in range(nq)]
                has_prev = i > 0
            else:
                kprev, vprev, has_prev = slabs(kc_ref, r, (b - 1) * rr), slabs(vc_ref, r, (b - 1) * rr), None
            kcat = jnp.concatenate(kprev + slabs(kc_ref, r, b * rr), axis=0)
            vcat = jnp.concatenate(vprev + slabs(vc_ref, r, b * rr), axis=0)
            q = jnp.concatenate(slabs(q_ref, r, b * rr), axis=0)

            def write(o, lse, r=r, b=b):
                for qq in range(nq):
                    dst = pl.ds(MAX_DILATION * b * rr + qq * dilation + r, rr, stride=MAX_DILATION)
                    for half in range(ATTN_GROUP_WIDTH // LANES):
                        ls = slice(half * LANES, (half + 1) * LANES)
                        o_ref[half, dst, :] = o[qq * rr:(qq + 1) * rr, ls]
                        l_ref[half, dst, :] = lse[qq * rr:(qq + 1) * rr, ls]

            blocks.append((q, kcat, vcat, has_prev, write))
        batches.append(blocks)
    _attn_tile(batches, pos_of, s_scr, p_scr)


def _dil_attn(qkv, col0, dilation, bsz, seq):
    gw, w = ATTN_GROUP_WIDTH, ATTN_BLOCK
    nt = seq // ATTN_TILE
    n_sc = ATTN_TILE // w * ATTN_HEADS_PER_GROUP
    scratch = [pltpu.VMEM((n_sc, w, 2 * w), F32), pltpu.VMEM((n_sc, w, 2 * w), BF16)]
    out_spec = pl.BlockSpec((gw // LANES, ATTN_TILE, LANES), lambda b, i: (0, b * nt + i, 0))
    out_shape = [jax.ShapeDtypeStruct((gw // LANES, bsz * seq, LANES), F32)] * 2
    if dilation == 1:
        bpt = ATTN_TILE // w
        cur = lambda c: pl.BlockSpec((ATTN_TILE, gw), lambda b, i: (b * nt + i, c))
        halo = lambda c: pl.BlockSpec((w, gw), lambda b, i: (jnp.maximum((b * nt + i) * bpt - 1, 0), c))
        body = _dil_attn_nat_body
    else:
        rr = w // (MAX_DILATION // dilation)
        bpt = w // rr
        cur = lambda c: pl.BlockSpec((None, MAX_DILATION, w, gw), lambda b, i: (b, 0, i, c))
        halo = lambda c: pl.BlockSpec((None, MAX_DILATION, rr, gw),
                                      lambda b, i: (b, 0, jnp.maximum(i * bpt - 1, 0), c))
        body = functools.partial(_dil_attn_res_body, dilation=dilation)
    return pl.pallas_call(
        body,
        grid=(bsz, nt),
        in_specs=[cur(col0), cur(col0 + 1), halo(col0 + 1), cur(col0 + 2), halo(col0 + 2)],
        out_specs=[out_spec, out_spec],
        out_shape=out_shape,
        scratch_shapes=scratch,
        compiler_params=_params(2),
        name=f"dil_attn_d{dilation}",
    )(qkv, qkv, qkv, qkv, qkv)


def _mem_kv_body(mem_ref, g_ref, w_ref, k_ref, v_ref):
    n = _rms(mem_ref[...], g_ref[...]).astype(BF16)
    kv = _dot(n, w_ref[...])
    k_ref[...] = kv[:, :MEM_WIDTH].astype(BF16)
    v_ref[...] = kv[:, MEM_WIDTH:].astype(BF16)


def _mem_kv(mem, g, w_kv):
    bsz, mem_len, _ = mem.shape
    blk = pl.BlockSpec((None, mem_len, MEM_WIDTH), lambda b: (b, 0, 0))
    return pl.pallas_call(
        _mem_kv_body,
        grid=(bsz,),
        in_specs=[pl.BlockSpec((None, mem_len, D_MODEL), lambda b: (b, 0, 0)),
                  _resident((1, D_MODEL)), _resident(w_kv.shape)],
        out_specs=[blk, blk],
        out_shape=[jax.ShapeDtypeStruct((bsz, mem_len, MEM_WIDTH), BF16)] * 2,
        compiler_params=_params(1),
        name="mem_kv",
    )(mem, g, w_kv)


def _merge_body(y_ref, o0_ref, o1_ref, o2_ref, l0_ref, l1_ref, l2_ref, mk_ref, mv_ref, x_ref,
                g1_ref, wmq_ref, wg_ref, wglut_ref, bglu_ref, wssm_ref, wattn_ref, wmem_ref, wo_ref,
                bgate_ref, h_ref):
    n = _rms(x_ref[...], g1_ref[...]).astype(BF16)

    hh = SSM_GROUP_SIZE
    yt = jnp.concatenate([y_ref[:, c * hh:(c + 1) * hh, :].reshape(SSM_WIDTH, SSM_CHUNK)
                          for c in range(y_ref.shape[1] // hh)], axis=1)
    gy = _twice_gelu_tanh(yt)
    glu = _dot(wglut_ref[...], gy.astype(BF16)) + bglu_ref[...]
    y2 = (gy * _twice_sigmoid_of_twice(glu)).T.astype(BF16)
    br_ssm = _dot(y2, wssm_ref[...])

    halves = []
    for half in range(ATTN_GROUP_WIDTH // LANES):
        l0, l1, l2 = l0_ref[half], l1_ref[half], l2_ref[half]
        mx = jnp.maximum(jnp.maximum(l0, l1), l2)
        e0, e1, e2 = jnp.exp(l0 - mx), jnp.exp(l1 - mx), jnp.exp(l2 - mx)
        halves.append((e0 * o0_ref[half] + e1 * o1_ref[half] + e2 * o2_ref[half]) / (e0 + e1 + e2))
    br_attn = _dot(jnp.concatenate(halves, axis=1).astype(BF16), wattn_ref[...])

    e = MEM_HEAD_DIM
    scale = e ** -0.5
    mq = _dot(n, wmq_ref[...]).astype(BF16)
    heads = []
    for h in range(MEM_HEADS):
        sl = slice(h * e, (h + 1) * e)
        s = _dot_nt(mq[:, sl], mk_ref[:, sl]) * scale
        p = jnp.exp(s - jnp.max(s, axis=-1, keepdims=True))
        inv = 1.0 / jnp.sum(p, axis=-1, keepdims=True)
        heads.append(_dot(p.astype(BF16), mv_ref[:, sl]) * inv)
    br_mem = _dot(jnp.concatenate(heads, axis=1).astype(BF16), wmem_ref[...])

    d = D_MODEL
    gate = lambda k: _twice_sigmoid_of_twice(
        _dot(n, wg_ref[:, k * d:(k + 1) * d]) + bgate_ref[:, k * d:(k + 1) * d])
    merged = gate(0) * br_ssm + gate(1) * br_attn + gate(2) * br_mem
    h_ref[...] = x_ref[...] + _dot(merged.astype(BF16), wo_ref[...])


def _merge(y3, os, ls, mk, mv, x2, weights, bsz, seq, tm):
    nt = seq // tm
    cpt = tm // SSM_CHUNK
    mem_len = mk.shape[1]
    row = lambda b, i: (b * nt + i, 0)
    tok = lambda width: pl.BlockSpec((tm, width), row)
    per_batch = pl.BlockSpec((None, mem_len, MEM_WIDTH), lambda b, i: (b, 0, 0))
    return pl.pallas_call(
        _merge_body,
        grid=(bsz, nt),
        in_specs=[pl.BlockSpec((SSM_GROUPS, cpt * SSM_GROUP_SIZE, SSM_CHUNK), lambda b, i: (0, b * nt + i, 0))]
        + [pl.BlockSpec((ATTN_GROUP_WIDTH // LANES, tm, LANES), lambda b, i: (0, b * nt + i, 0))] * 6
        + [per_batch, per_batch, tok(D_MODEL)]
        + [_resident(w.shape) for w in weights],
        out_specs=tok(D_MODEL),
        out_shape=jax.ShapeDtypeStruct((bsz * seq, D_MODEL), F32),
        compiler_params=_params(2),
        name="merge",
    )(y3, *os, *ls, mk, mv, x2, *weights)


def _mlp_body(h_ref, g2_ref, wup_ref, wdown_ref, gf_ref, out_ref, *, final_norm, ff_chunk):
    h = h_ref[...]
    n2 = _rms(h, g2_ref[...]).astype(BF16)
    acc = h
    for c in range(0, D_FF, ff_chunk):
        a = jnp.maximum(_dot(n2, wup_ref[:, c:c + ff_chunk]), 0.0)
        acc = acc + _dot((a * a).astype(BF16), wdown_ref[c:c + ff_chunk, :])
    out_ref[...] = _rms(acc, gf_ref[...]) if final_norm else acc


def _mlp(h, g2, w_up, w_down, gf, final_norm, tm):
    tokens = h.shape[0]
    row = lambda i: (i, 0)
    return pl.pallas_call(
        functools.partial(_mlp_body, final_norm=final_norm, ff_chunk=D_MODEL),
        grid=(tokens // tm,),
        in_specs=[pl.BlockSpec((tm, D_MODEL), row), _resident((1, D_MODEL)),
                  _resident(w_up.shape), _resident(w_down.shape), _resident((1, D_MODEL))],
        out_specs=pl.BlockSpec((tm, D_MODEL), row),
        out_shape=jax.ShapeDtypeStruct((tokens, D_MODEL), F32),
        compiler_params=_params(1),
        name="mlp",
    )(h, g2, w_up, w_down, gf)


def kernel(x, mem, norm1_g, mem_norm_g, w_in, b_gate, ssm_lambda_re, ssm_lambda_im, ssm_log_dt, ssm_b_re, ssm_b_im, ssm_c_re, ssm_c_im, ssm_d, w_glu, b_glu, w_ssm_br, w_attn_br, w_mem_kv, w_mem_br, w_o, norm2_g, w_up, w_down, final_g):
    bsz, seq, d_model = x.shape
    depth = w_in.shape[0]
    assert d_model == D_MODEL and seq % ATTN_TILE == 0
    assert tuple(w // d for w, d in ATTN_PATTERNS) == (ATTN_BLOCK,) * 3
    assert tuple(d for _, d in ATTN_PATTERNS) == (1, 4, 16)
    tm = 512
    h = x.reshape(bsz * seq, D_MODEL)
    row = lambda v: v.reshape(1, -1).astype(F32)
    aw, gw = ATTN_WIDTH, ATTN_GROUP_WIDTH
    o_q, o_k, o_v = SSM_WIDTH, SSM_WIDTH + aw, SSM_WIDTH + 2 * aw
    o_mq = SSM_WIDTH + 3 * aw
    o_g = o_mq + MEM_WIDTH
    for i in range(depth):
        wi = w_in[i]
        cols = lambda g: [wi[:, o + g * gw:o + (g + 1) * gw] for o in (o_q, o_k, o_v)]
        w_qkv0 = jnp.concatenate(cols(0), axis=1).astype(BF16)
        w_qkvd = jnp.concatenate(cols(1) + cols(2), axis=1).astype(BF16)
        u3, qkv0, qkvd = _in_proj(
            h, row(norm1_g[i]), wi[:, :SSM_WIDTH].T.astype(BF16), w_qkv0, w_qkvd, bsz, seq, tm)

        ops = _ssm_operators(ssm_lambda_re[i], ssm_lambda_im[i], ssm_log_dt[i], ssm_b_re[i],
                             ssm_b_im[i], ssm_c_re[i], ssm_c_im[i], ssm_d[i])
        y3 = _ssm(u3, *ops, bsz)

        outs = [_dil_attn(qkv0, 0, 1, bsz, seq), _dil_attn(qkvd, 0, 4, bsz, seq),
                _dil_attn(qkvd, 3, 16, bsz, seq)]
        os, ls = [o for o, _ in outs], [l for _, l in outs]

        mk, mv = _mem_kv(mem, row(mem_norm_g[i]), w_mem_kv[i].astype(BF16))

        merge_weights = [
            row(norm1_g[i]), wi[:, o_mq:o_g].astype(BF16), (0.5 * wi[:, o_g:]).astype(BF16),
            (0.25 * w_glu[i]).T.astype(BF16), (0.5 * b_glu[i]).reshape(-1, 1).astype(F32),
            (0.125 * w_ssm_br[i]).astype(BF16), (0.5 * w_attn_br[i]).astype(BF16),
            (0.5 * w_mem_br[i]).astype(BF16), w_o[i].astype(BF16), row(0.5 * b_gate[i])]
        h = _merge(y3, os, ls, mk, mv, h, merge_weights, bsz, seq, tm)

        h = _mlp(h, row(norm2_g[i]), w_up[i].astype(BF16), w_down[i].astype(BF16), row(final_g),
                 final_norm=(i == depth - 1), tm=2 * tm)
    return h.reshape(bsz, seq, D_MODEL)
```

```python
import functools
import math

import jax
import jax.numpy as jnp
from jax import lax
from jax.experimental import pallas as pl
from jax.experimental.pallas import tpu as pltpu

D_MODEL = 1024
SSM_GROUP_SIZE = 16
SSM_GROUPS = 32
SSM_WIDTH = SSM_GROUP_SIZE * SSM_GROUPS
SSM_STATE = 64
SSM_CHUNK = 128
SSM_PANEL = 2
ATTN_PATTERNS = ((128, 1), (512, 4), (2048, 16))
ATTN_HEADS_PER_GROUP = 4
ATTN_HEAD_DIM = 64
ATTN_GROUP_WIDTH = ATTN_HEADS_PER_GROUP * ATTN_HEAD_DIM
ATTN_WIDTH = ATTN_GROUP_WIDTH * len(ATTN_PATTERNS)
ATTN_BLOCK = 128
MAX_DILATION = 16
ATTN_TILE = ATTN_BLOCK * MAX_DILATION
ATTN_BATCH = 1
MEM_HEADS = 4
MEM_HEAD_DIM = 128
MEM_WIDTH = MEM_HEADS * MEM_HEAD_DIM
N_BRANCHES = 3
D_FF = 4 * D_MODEL
RMS_EPS = 1e-6
NEG_INF = -1e30

TOKEN_TILE = 512
MLP_TOKEN_TILE = 1024
LANES = 128

VMEM_LIMIT_BYTES = 56 * 1024 * 1024

F32 = jnp.float32
BF16 = jnp.bfloat16


def _params(n_axes):
    return pltpu.CompilerParams(
        dimension_semantics=("arbitrary",) * n_axes, vmem_limit_bytes=VMEM_LIMIT_BYTES)


def _resident(shape):
    nd = len(shape)
    return pl.BlockSpec(shape, lambda *_: (0,) * nd, pipeline_mode=pl.Buffered(1))


def _rms(x, g):
    ms = jnp.mean(x * x, axis=-1, keepdims=True)
    return x * lax.rsqrt(ms + RMS_EPS) * g


def _twice_sigmoid_of_twice(x):
    return 1.0 + jnp.tanh(x)


def _twice_gelu_tanh(x):
    c = math.sqrt(2.0 / math.pi)
    return x * (1.0 + jnp.tanh(x * (c + (c * 0.044715) * (x * x))))


def _dot(a, b):
    return jnp.dot(a, b, preferred_element_type=F32)


def _dot_nt(a, b):
    return lax.dot_general(a, b, (((1,), (1,)), ((), ())), preferred_element_type=F32)


def _in_proj_body(x_ref, g_ref, wut_ref, wqkv0_ref, wqkvd_ref, u_ref, qkv0_ref, qkvd_ref, n_scr):
    tm = x_ref.shape[0]
    nf = _rms(x_ref[...], g_ref[...])
    n = nf.astype(BF16)
    for c in range(D_MODEL // LANES):
        n_scr[c] = nf[:, c * LANES:(c + 1) * LANES]
    ut = _dot_nt(wut_ref[...], n)
    for c in range(tm // SSM_CHUNK):
        u_ref[:, c * SSM_GROUP_SIZE:(c + 1) * SSM_GROUP_SIZE, :] = (
            ut[:, c * SSM_CHUNK:(c + 1) * SSM_CHUNK].reshape(SSM_GROUPS, SSM_GROUP_SIZE, SSM_CHUNK))
    qkv0_ref[...] = _dot(n, wqkv0_ref[...]).astype(BF16)
    rows = tm // MAX_DILATION
    n_perm = jnp.concatenate(
        [jnp.concatenate([n_scr[c, pl.ds(r, rows, stride=MAX_DILATION), :]
                          for c in range(D_MODEL // LANES)], axis=1)
         for r in range(MAX_DILATION)], axis=0)
    zp = _dot(n_perm.astype(BF16), wqkvd_ref[...]).astype(BF16)
    for r in range(MAX_DILATION):
        qkvd_ref[r] = zp[r * rows:(r + 1) * rows]


def _in_proj(x2, g, w_ut, w_qkv0, w_qkvd, bsz, seq, tm):
    tokens = bsz * seq
    nt = seq // tm
    cpt = tm // SSM_CHUNK
    n_chunks = seq // SSM_CHUNK
    wd = w_qkvd.shape[1]
    row = lambda b, i: (b * nt + i, 0)
    return pl.pallas_call(
        _in_proj_body,
        grid=(bsz, nt),
        in_specs=[
            pl.BlockSpec((tm, D_MODEL), row),
            _resident((1, D_MODEL)),
            _resident(w_ut.shape), _resident(w_qkv0.shape), _resident(w_qkvd.shape),
        ],
        out_specs=[
            pl.BlockSpec((SSM_GROUPS, cpt * SSM_GROUP_SIZE, SSM_CHUNK), lambda b, i: (0, b * nt + i, 0)),
            pl.BlockSpec((tm, ATTN_WIDTH), row),
            pl.BlockSpec((None, MAX_DILATION, tm // MAX_DILATION, wd), lambda b, i: (b, 0, i, 0)),
        ],
        out_shape=[
            jax.ShapeDtypeStruct((SSM_GROUPS, bsz * n_chunks * SSM_GROUP_SIZE, SSM_CHUNK), F32),
            jax.ShapeDtypeStruct((tokens, ATTN_WIDTH), BF16),
            jax.ShapeDtypeStruct((bsz, MAX_DILATION, seq // MAX_DILATION, wd), BF16),
        ],
        scratch_shapes=[pltpu.VMEM((D_MODEL // LANES, tm, LANES), F32)],
        compiler_params=_params(2),
        name="in_proj",
    )(x2, g, w_ut, w_qkv0, w_qkvd)


def _ssm_body(u_ref, pk_ref, pq_ref, bc_ref, dtap_ref, y_ref,
              k_scr, t_scr, m_scr, nt_scr, z_scr, sp_scr, *, bsz):
    hh = SSM_GROUP_SIZE
    cc = u_ref.shape[1]
    rows = u_ref.shape[0] // hh
    n_chunks = rows // bsz
    half = 2 * SSM_STATE

    q_re, q_im = pq_ref[0], pq_ref[1]
    r_re, r_im = pk_ref[0, 1:cc + 1, :], pk_ref[1, 1:cc + 1, :]
    b_re, b_im, c_re, c_im = bc_ref[0], bc_ref[1], bc_ref[2], bc_ref[3]
    w_re, w_im = [], []
    for h in range(hh):
        rs = slice(h * cc, (h + 1) * cc)
        bre, bim, cre, cim = b_re[h:h + 1], b_im[h:h + 1], c_re[h:h + 1], c_im[h:h + 1]
        m_scr[rs, 0:half] = (q_re * bre - q_im * bim).astype(BF16)
        m_scr[rs, half:2 * half] = (q_re * bim + q_im * bre).astype(BF16)
        nt_scr[rs, 0:half] = (r_re * cre - r_im * cim).astype(BF16)
        nt_scr[rs, half:2 * half] = (-(r_im * cre + r_re * cim)).astype(BF16)
        w_re.append(c_re * bre - c_im * bim)
        w_im.append(c_re * bim + c_im * bre)
    taps_dot = lambda w, p: lax.dot_general(
        w, p, (((1,), (1,)), ((), ())), precision=lax.Precision.HIGHEST, preferred_element_type=F32)
    k_scr[...] = (taps_dot(jnp.concatenate(w_re, axis=0), pk_ref[0, 0:cc, :])
                  - taps_dot(jnp.concatenate(w_im, axis=0), pk_ref[1, 0:cc, :])
                  + dtap_ref[...])

    srow = lax.broadcasted_iota(jnp.int32, (cc, cc), 0)
    tcol = lax.broadcasted_iota(jnp.int32, (cc, cc), 1)
    causal = tcol >= srow

    x = jnp.concatenate([u_ref[pl.ds(h, rows, stride=hh), :].astype(BF16) for h in range(hh)],
                        axis=1)
    z_scr[...] = _dot(x, m_scr[...]).reshape(bsz, n_chunks, 2 * half)
    ar = pk_ref[0, cc:cc + 1, :].reshape(1, 1, half)
    ai = pk_ref[1, cc:cc + 1, :].reshape(1, 1, half)
    sr = jnp.zeros((bsz, 1, half), F32)
    si = jnp.zeros((bsz, 1, half), F32)
    for c in range(n_chunks):
        sp_scr[:, c:c + 1, 0:half] = sr
        sp_scr[:, c:c + 1, half:2 * half] = si
        zr = z_scr[:, c:c + 1, 0:half]
        zi = z_scr[:, c:c + 1, half:2 * half]
        sr, si = ar * sr - ai * si + zr, ar * si + ai * sr + zi
    sp = sp_scr[...].reshape(rows, 2 * half).astype(BF16)

    for p0 in range(0, hh, SSM_PANEL):
        cols = slice(p0 * cc, (p0 + SSM_PANEL) * cc)
        for h in range(hh):
            for hp in range(p0, p0 + SSM_PANEL):
                kb = jnp.broadcast_to(k_scr[h * hh + hp:h * hh + hp + 1, :], (cc, cc))
                blk = pltpu.roll(kb, 0, 1, stride=1, stride_axis=0)
                t_scr[h * cc:(h + 1) * cc, hp * cc:(hp + 1) * cc] = (
                    jnp.where(causal, blk, 0.0).astype(BF16))
        y = _dot(x, t_scr[:, cols]) + _dot_nt(sp, nt_scr[cols, :])
        for hp in range(p0, p0 + SSM_PANEL):
            y_ref[pl.ds(hp, rows, stride=hh), :] = y[:, (hp - p0) * cc:(hp - p0 + 1) * cc]


def _ssm(u3, pk, pq, bc, dtap, bsz):
    hh = SSM_GROUP_SIZE
    _, rows16, cc = u3.shape
    n_chunks = rows16 // hh // bsz
    width = hh * cc
    half = 2 * SSM_STATE
    blk = pl.BlockSpec((None, rows16, cc), lambda g: (g, 0, 0))
    per_group = lambda a: pl.BlockSpec((None,) + a.shape[1:], lambda g: (g,) + (0,) * (a.ndim - 1))
    return pl.pallas_call(
        functools.partial(_ssm_body, bsz=bsz),
        grid=(SSM_GROUPS,),
        in_specs=[blk, per_group(pk), per_group(pq), per_group(bc), per_group(dtap)],
        out_specs=blk,
        out_shape=jax.ShapeDtypeStruct(u3.shape, F32),
        scratch_shapes=[
            pltpu.VMEM((hh * hh, cc), F32),
            pltpu.VMEM((width, width), BF16),
            pltpu.VMEM((width, 2 * half), BF16),
            pltpu.VMEM((width, 2 * half), BF16),
            pltpu.VMEM((bsz, n_chunks, 2 * half), F32),
            pltpu.VMEM((bsz, n_chunks, 2 * half), F32),
        ],
        compiler_params=_params(1),
        name="ssm",
    )(u3, pk, pq, bc, dtap)


def _ssm_operators(lam_re, lam_im, log_dt, b_re, b_im, c_re, c_im, d_skip):
    cc, pp, hh = SSM_CHUNK, SSM_STATE, SSM_GROUP_SIZE
    lr, li = lam_re.astype(F32), lam_im.astype(F32)
    dt = jnp.exp(log_dt.astype(F32))[:, None]
    mag = jnp.exp(lr * dt)
    a_re, a_im = mag * jnp.cos(li * dt), mag * jnp.sin(li * dt)
    nr, ni = a_re - 1.0, a_im
    den = lr * lr + li * li
    coef_re = (nr * lr + ni * li) / den
    coef_im = (ni * lr - nr * li) / den
    br_, bi_ = b_re.astype(F32), b_im.astype(F32)
    bb_re = coef_re[..., None] * br_ - coef_im[..., None] * bi_
    bb_im = coef_re[..., None] * bi_ + coef_im[..., None] * br_
    cr, ci = c_re.astype(F32), c_im.astype(F32)
    dd = d_skip.astype(F32)

    lane_pad = lambda v: jnp.pad(v, [(0, 0)] * (v.ndim - 1) + [(0, pp)])
    log_re, log_im = lane_pad(lr * dt), lane_pad(li * dt)
    live = (jnp.arange(2 * pp) < pp).astype(F32)

    def powers(ks):
        mag_k = jnp.exp(ks[None, :, None] * log_re[:, None, :]) * live[None, None, :]
        ang_k = ks[None, :, None] * log_im[:, None, :]
        return jnp.stack([mag_k * jnp.cos(ang_k), mag_k * jnp.sin(ang_k)], axis=1)

    pk = powers(jnp.arange(cc + 8, dtype=F32))
    pq = powers(cc - 1 - jnp.arange(cc, dtype=F32))
    bc = jnp.stack([lane_pad(bb_re.transpose(0, 2, 1)), lane_pad(bb_im.transpose(0, 2, 1)),
                    lane_pad(cr), lane_pad(ci)], axis=1)
    pair = jnp.arange(hh * hh)
    on_diag = jnp.logical_and((pair // hh == pair % hh)[:, None], (jnp.arange(cc) == 0)[None, :])
    dtap = jnp.repeat(dd, hh, axis=1)[:, :, None] * on_diag.astype(F32)[None]
    return pk, pq, bc, dtap


def _attn_tile(batches, pos_of, s_scr, p_scr):
    w, e, nh = ATTN_BLOCK, ATTN_HEAD_DIM, ATTN_HEADS_PER_GROUP
    lane = lax.broadcasted_iota(jnp.int32, (w, 2 * w), 1)
    pkey = pos_of(lane & (w - 1))
    pqry = pos_of(lax.broadcasted_iota(jnp.int32, (w, 2 * w), 0))
    mask = jnp.logical_or(jnp.logical_and(lane < w, pkey >= pqry),
                          jnp.logical_and(lane >= w, pkey <= pqry))
    head_of_lane = lane // e
    scale = e ** -0.5
    slot0 = [0]
    for blocks in batches:
        slot0.append(slot0[-1] + len(blocks) * nh)

    def scores(k):
        for bi, (q, kcat, _, has_prev, _) in enumerate(batches[k]):
            qs = q * scale
            qm = jnp.concatenate([jnp.where(head_of_lane == h, qs, 0) for h in range(nh)], axis=0)
            s = _dot_nt(qm, kcat).reshape(nh, w, 2 * w)
            if has_prev is not None:
                s = jnp.where(jnp.logical_or(lane >= w, has_prev)[None], s, NEG_INF)
            s_scr[slot0[k] + bi * nh:slot0[k] + (bi + 1) * nh] = s

    def softmax(k):
        s_all = jnp.where(mask[None], s_scr[slot0[k]:slot0[k + 1]], NEG_INF)
        mx = jnp.max(s_all, axis=-1, keepdims=True)
        p = jnp.exp(s_all - mx)
        den = jnp.sum(p, axis=-1, keepdims=True)
        p_scr[slot0[k]:slot0[k + 1]] = p.astype(BF16)
        return mx, den

    def values(k, mx, den):
        for bi, (_, _, vcat, _, write) in enumerate(batches[k]):
            lo = slot0[k] + bi * nh
            pv = _dot(p_scr[lo:lo + nh].reshape(nh * w, 2 * w), vcat).reshape(nh, w, nh * e)
            wide = lambda st: jnp.concatenate(
                [jnp.broadcast_to(st[bi * nh + h], (w, e)) for h in range(nh)], axis=1)
            den_t = wide(den)
            num = jnp.concatenate([pv[h][:, h * e:(h + 1) * e] for h in range(nh)], axis=1)
            write(num * (1.0 / den_t), wide(mx) + jnp.log(den_t))

    scores(0)
    for k in range(len(batches)):
        if k + 1 < len(batches):
            scores(k + 1)
        values(k, *softmax(k))


def _dil_attn_nat_body(q_ref, kc_ref, kh_ref, vc_ref, vh_ref, o_ref, l_ref, s_scr, p_scr):
    i = pl.program_id(1)
    w = ATTN_BLOCK
    pos_of = lambda idx: idx
    n_blk = q_ref.shape[0] // w
    batches = []
    for j0 in range(0, n_blk, ATTN_BATCH):
        blocks = []
        for j in range(j0, j0 + ATTN_BATCH):
            rs = slice(j * w, (j + 1) * w)
            if j == 0:
                kcat = jnp.concatenate([kh_ref[...], kc_ref[rs, :]], axis=0)
                vcat = jnp.concatenate([vh_ref[...], vc_ref[rs, :]], axis=0)
                has_prev = i > 0
            else:
                kcat, vcat, has_prev = kc_ref[(j - 1) * w:(j + 1) * w, :], vc_ref[(j - 1) * w:(j + 1) * w, :], None

            def write(o, lse, rs=rs):
                for half in range(ATTN_GROUP_WIDTH // LANES):
                    ls = slice(half * LANES, (half + 1) * LANES)
                    o_ref[half, rs, :] = o[:, ls]
                    l_ref[half, rs, :] = lse[:, ls]

            blocks.append((q_ref[rs, :], kcat, vcat, has_prev, write))
        batches.append(blocks)
    _attn_tile(batches, pos_of, s_scr, p_scr)


def _dil_attn_res_body(q_ref, kc_ref, kh_ref, vc_ref, vh_ref, o_ref, l_ref, s_scr, p_scr, *, dilation):
    i = pl.program_id(1)
    w = ATTN_BLOCK
    nq = MAX_DILATION // dilation
    rr = w // nq
    shift = rr.bit_length() - 1
    pos_of = lambda idx: nq * (idx & (rr - 1)) + (idx >> shift)

    def slabs(ref, r, lo):
        return [ref[q * dilation + r, lo:lo + rr, :] for q in range(nq)]

    todo = [(r, b) for r in range(dilation) for b in range(nq)]
    batches = []
    for j0 in range(0, len(todo), ATTN_BATCH):
        blocks = []
        for r, b in todo[j0:j0 + ATTN_BATCH]:
            if b == 0:
                kprev = [kh_ref[q * dilation + r] for q in range(nq)]
                vprev = [vh_ref[q * dilation + r] for q in range(nq)]
                has_prev = i > 0
            else:
                kprev, vprev, has_prev = slabs(kc_ref, r, (b - 1) * rr), slabs(vc_ref, r, (b - 1) * rr), None
            kcat = jnp.concatenate(kprev + slabs(kc_ref, r, b * rr), axis=0)
            vcat = jnp.concatenate(vprev + slabs(vc_ref, r, b * rr), axis=0)
            q = jnp.concatenate(slabs(q_ref, r, b * rr), axis=0)

            def write(o, lse, r=r, b=b):
                for qq in range(nq):
                    dst = pl.ds(MAX_DILATION * b * rr + qq * dilation + r, rr, stride=MAX_DILATION)
                    for half in range(ATTN_GROUP_WIDTH // LANES):
                        ls = slice(half * LANES, (half + 1) * LANES)
                        o_ref[half, dst, :] = o[qq * rr:(qq + 1) * rr, ls]
                        l_ref[half, dst, :] = lse[qq * rr:(qq + 1) * rr, ls]

            blocks.append((q, kcat, vcat, has_prev, write))
        batches.append(blocks)
    _attn_tile(batches, pos_of, s_scr, p_scr)


def _dil_attn(qkv, col0, dilation, bsz, seq):
    gw, w = ATTN_GROUP_WIDTH, ATTN_BLOCK
    nt = seq // ATTN_TILE
    n_sc = ATTN_TILE // w * ATTN_HEADS_PER_GROUP
    scratch = [pltpu.VMEM((n_sc, w, 2 * w), F32), pltpu.VMEM((n_sc, w, 2 * w), BF16)]
    out_spec = pl.BlockSpec((gw // LANES, ATTN_TILE, LANES), lambda b, i: (0, b * nt + i, 0))
    out_shape = [jax.ShapeDtypeStruct((gw // LANES, bsz * seq, LANES), F32)] * 2
    if dilation == 1:
        bpt = ATTN_TILE // w
        cur = lambda c: pl.BlockSpec((ATTN_TILE, gw), lambda b, i: (b * nt + i, c))
        halo = lambda c: pl.BlockSpec((w, gw), lambda b, i: (jnp.maximum((b * nt + i) * bpt - 1, 0), c))
        body = _dil_attn_nat_body
    else:
        rr = w // (MAX_DILATION // dilation)
        bpt = w // rr
        cur = lambda c: pl.BlockSpec((None, MAX_DILATION, w, gw), lambda b, i: (b, 0, i, c))
        halo = lambda c: pl.BlockSpec((None, MAX_DILATION, rr, gw),
                                      lambda b, i: (b, 0, jnp.maximum(i * bpt - 1, 0), c))
        body = functools.partial(_dil_attn_res_body, dilation=dilation)
    return pl.pallas_call(
        body,
        grid=(bsz, nt),
        in_specs=[cur(col0), cur(col0 + 1), halo(col0 + 1), cur(col0 + 2), halo(col0 + 2)],
        out_specs=[out_spec, out_spec],
        out_shape=out_shape,
        scratch_shapes=scratch,
        compiler_params=_params(2),
        name=f"dil_attn_d{dilation}",
    )(qkv, qkv, qkv, qkv, qkv)


def _mem_kv_body(mem_ref, g_ref, w_ref, k_ref, v_ref):
    n = _rms(mem_ref[...], g_ref[...]).astype(BF16)
    kv = _dot(n, w_ref[...])
    k_ref[...] = kv[:, :MEM_WIDTH].astype(BF16)
    v_ref[...] = kv[:, MEM_WIDTH:].astype(BF16)


def _mem_kv(mem, g, w_kv):
    bsz, mem_len, _ = mem.shape
    blk = pl.BlockSpec((None, mem_len, MEM_WIDTH), lambda b: (b, 0, 0))
    return pl.pallas_call(
        _mem_kv_body,
        grid=(bsz,),
        in_specs=[pl.BlockSpec((None, mem_len, D_MODEL), lambda b: (b, 0, 0)),
                  _resident((1, D_MODEL)), _resident(w_kv.shape)],
        out_specs=[blk, blk],
        out_shape=[jax.ShapeDtypeStruct((bsz, mem_len, MEM_WIDTH), BF16)] * 2,
        compiler_params=_params(1),
        name="mem_kv",
    )(mem, g, w_kv)


def _merge_body(y_ref, o0_ref, o1_ref, o2_ref, l0_ref, l1_ref, l2_ref, mk_ref, mv_ref, x_ref,
                g1_ref, wmq_ref, wg_ref, wglut_ref, bglu_ref, wssm_ref, wattn_ref, wmem_ref, wo_ref,
                bgate_ref, h_ref):
    n = _rms(x_ref[...], g1_ref[...]).astype(BF16)

    hh = SSM_GROUP_SIZE
    yt = jnp.concatenate([y_ref[:, c * hh:(c + 1) * hh, :].reshape(SSM_WIDTH, SSM_CHUNK)
                          for c in range(y_ref.shape[1] // hh)], axis=1)
    gy = _twice_gelu_tanh(yt)
    glu = _dot(wglut_ref[...], gy.astype(BF16)) + bglu_ref[...]
    y2 = (gy * _twice_sigmoid_of_twice(glu)).T.astype(BF16)
    br_ssm = _dot(y2, wssm_ref[...])

    halves = []
    for half in range(ATTN_GROUP_WIDTH // LANES):
        l0, l1, l2 = l0_ref[half], l1_ref[half], l2_ref[half]
        mx = jnp.maximum(jnp.maximum(l0, l1), l2)
        e0, e1, e2 = jnp.exp(l0 - mx), jnp.exp(l1 - mx), jnp.exp(l2 - mx)
        halves.append((e0 * o0_ref[half] + e1 * o1_ref[half] + e2 * o2_ref[half]) / (e0 + e1 + e2))
    br_attn = _dot(jnp.concatenate(halves, axis=1).astype(BF16), wattn_ref[...])

    e = MEM_HEAD_DIM
    scale = e ** -0.5
    mq = _dot(n, wmq_ref[...]).astype(BF16)
    heads = []
    for h in range(MEM_HEADS):
        sl = slice(h * e, (h + 1) * e)
        s = _dot_nt(mq[:, sl], mk_ref[:, sl]) * scale
        p = jnp.exp(s - jnp.max(s, axis=-1, keepdims=True))
        inv = 1.0 / jnp.sum(p, axis=-1, keepdims=True)
        heads.append(_dot(p.astype(BF16), mv_ref[:, sl]) * inv)
    br_mem = _dot(jnp.concatenate(heads, axis=1).astype(BF16), wmem_ref[...])

    d = D_MODEL
    gate = lambda k: _twice_sigmoid_of_twice(
        _dot(n, wg_ref[:, k * d:(k + 1) * d]) + bgate_ref[:, k * d:(k + 1) * d])
    merged = gate(0) * br_ssm + gate(1) * br_attn + gate(2) * br_mem
    h_ref[...] = x_ref[...] + _dot(merged.astype(BF16), wo_ref[...])


def _merge(y3, os, ls, mk, mv, x2, weights, bsz, seq, tm):
    nt = seq // tm
    cpt = tm // SSM_CHUNK
    mem_len = mk.shape[1]
    row = lambda b, i: (b * nt + i, 0)
    tok = lambda width: pl.BlockSpec((tm, width), row)
    per_batch = pl.BlockSpec((None, mem_len, MEM_WIDTH), lambda b, i: (b, 0, 0))
    return pl.pallas_call(
        _merge_body,
        grid=(bsz, nt),
        in_specs=[pl.BlockSpec((SSM_GROUPS, cpt * SSM_GROUP_SIZE, SSM_CHUNK), lambda b, i: (0, b * nt + i, 0))]
        + [pl.BlockSpec((ATTN_GROUP_WIDTH // LANES, tm, LANES), lambda b, i: (0, b * nt + i, 0))] * 6
        + [per_batch, per_batch, tok(D_MODEL)]
        + [_resident(w.shape) for w in weights],
        out_specs=tok(D_MODEL),
        out_shape=jax.ShapeDtypeStruct((bsz * seq, D_MODEL), F32),
        compiler_params=_params(2),
        name="merge",
    )(y3, *os, *ls, mk, mv, x2, *weights)


def _mlp_body(h_ref, g2_ref, wup_ref, wdown_ref, gf_ref, out_ref, *, final_norm, ff_chunk):
    h = h_ref[...]
    n2 = _rms(h, g2_ref[...]).astype(BF16)
    acc = h
    for c in range(0, D_FF, ff_chunk):
        a = jnp.maximum(_dot(n2, wup_ref[:, c:c + ff_chunk]), 0.0)
        acc = acc + _dot((a * a).astype(BF16), wdown_ref[c:c + ff_chunk, :])
    out_ref[...] = _rms(acc, gf_ref[...]) if final_norm else acc


def _mlp(h, g2, w_up, w_down, gf, final_norm, tm):
    tokens = h.shape[0]
    row = lambda i: (i, 0)
    return pl.pallas_call(
        functools.partial(_mlp_body, final_norm=final_norm, ff_chunk=D_MODEL),
        grid=(tokens // tm,),
        in_specs=[pl.BlockSpec((tm, D_MODEL), row), _resident((1, D_MODEL)),
                  _resident(w_up.shape), _resident(w_down.shape), _resident((1, D_MODEL))],
        out_specs=pl.BlockSpec((tm, D_MODEL), row),
        out_shape=jax.ShapeDtypeStruct((tokens, D_MODEL), F32),
        compiler_params=_params(1),
        name="mlp",
    )(h, g2, w_up, w_down, gf)


def kernel(x, mem, norm1_g, mem_norm_g, w_in, b_gate, ssm_lambda_re, ssm_lambda_im, ssm_log_dt, ssm_b_re, ssm_b_im, ssm_c_re, ssm_c_im, ssm_d, w_glu, b_glu, w_ssm_br, w_attn_br, w_mem_kv, w_mem_br, w_o, norm2_g, w_up, w_down, final_g):
    bsz, seq, d_model = x.shape
    depth = w_in.shape[0]
    assert d_model == D_MODEL and seq % ATTN_TILE == 0
    assert tuple(w // d for w, d in ATTN_PATTERNS) == (ATTN_BLOCK,) * 3
    assert tuple(d for _, d in ATTN_PATTERNS) == (1, 4, 16)
    tm = TOKEN_TILE
    h = x.reshape(bsz * seq, D_MODEL)
    row = lambda v: v.reshape(1, -1).astype(F32)
    aw, gw = ATTN_WIDTH, ATTN_GROUP_WIDTH
    o_q, o_k, o_v = SSM_WIDTH, SSM_WIDTH + aw, SSM_WIDTH + 2 * aw
    o_mq = SSM_WIDTH + 3 * aw
    o_g = o_mq + MEM_WIDTH
    for i in range(depth):
        wi = w_in[i]
        cols = lambda g: [wi[:, o + g * gw:o + (g + 1) * gw] for o in (o_q, o_k, o_v)]
        w_qkv0 = jnp.concatenate(cols(0), axis=1).astype(BF16)
        w_qkvd = jnp.concatenate(cols(1) + cols(2), axis=1).astype(BF16)
        u3, qkv0, qkvd = _in_proj(
            h, row(norm1_g[i]), wi[:, :SSM_WIDTH].T.astype(BF16), w_qkv0, w_qkvd, bsz, seq, tm)

        ops = _ssm_operators(ssm_lambda_re[i], ssm_lambda_im[i], ssm_log_dt[i], ssm_b_re[i],
                             ssm_b_im[i], ssm_c_re[i], ssm_c_im[i], ssm_d[i])
        y3 = _ssm(u3, *ops, bsz)

        outs = [_dil_attn(qkv0, 0, 1, bsz, seq), _dil_attn(qkvd, 0, 4, bsz, seq),
                _dil_attn(qkvd, 3, 16, bsz, seq)]
        os, ls = [o for o, _ in outs], [l for _, l in outs]

        mk, mv = _mem_kv(mem, row(mem_norm_g[i]), w_mem_kv[i].astype(BF16))

        merge_weights = [
            row(norm1_g[i]), wi[:, o_mq:o_g].astype(BF16), (0.5 * wi[:, o_g:]).astype(BF16),
            (0.25 * w_glu[i]).T.astype(BF16), (0.5 * b_glu[i]).reshape(-1, 1).astype(F32),
            (0.125 * w_ssm_br[i]).astype(BF16), (0.5 * w_attn_br[i]).astype(BF16),
            (0.5 * w_mem_br[i]).astype(BF16), w_o[i].astype(BF16), row(0.5 * b_gate[i])]
        h = _merge(y3, os, ls, mk, mv, h, merge_weights, bsz, seq, tm)

        h = _mlp(h, row(norm2_g[i]), w_up[i].astype(BF16), w_down[i].astype(BF16), row(final_g),
                 final_norm=(i == depth - 1), tm=MLP_TOKEN_TILE)
    return h.reshape(bsz, seq, D_MODEL)
```

```python
import functools
import math

import jax
import jax.numpy as jnp
from jax import lax
from jax.experimental import pallas as pl
from jax.experimental.pallas import tpu as pltpu

D_MODEL = 1024
SSM_GROUP_SIZE = 16
SSM_GROUPS = 32
SSM_WIDTH = SSM_GROUP_SIZE * SSM_GROUPS
SSM_STATE = 64
SSM_CHUNK = 128
SSM_PANEL = 2
ATTN_PATTERNS = ((128, 1), (512, 4), (2048, 16))
ATTN_HEADS_PER_GROUP = 4
ATTN_HEAD_DIM = 64
ATTN_GROUP_WIDTH = ATTN_HEADS_PER_GROUP * ATTN_HEAD_DIM
ATTN_WIDTH = ATTN_GROUP_WIDTH * len(ATTN_PATTERNS)
ATTN_BLOCK = 128
MAX_DILATION = 16
ATTN_TILE = ATTN_BLOCK * MAX_DILATION
ATTN_BATCH = 1
MEM_HEADS = 4
MEM_HEAD_DIM = 128
MEM_WIDTH = MEM_HEADS * MEM_HEAD_DIM
N_BRANCHES = 3
D_FF = 4 * D_MODEL
RMS_EPS = 1e-6
NEG_INF = -1e30

TOKEN_TILE = 512
MLP_TOKEN_TILE = 1024
LANES = 128

VMEM_LIMIT_BYTES = 56 * 1024 * 1024

F32 = jnp.float32
BF16 = jnp.bfloat16


def _params(n_axes):
    return pltpu.CompilerParams(
        dimension_semantics=("arbitrary",) * n_axes, vmem_limit_bytes=VMEM_LIMIT_BYTES)


def _resident(shape):
    nd = len(shape)
    return pl.BlockSpec(shape, lambda *_: (0,) * nd, pipeline_mode=pl.Buffered(1))


def _rms(x, g):
    ms = jnp.mean(x * x, axis=-1, keepdims=True)
    return x * lax.rsqrt(ms + RMS_EPS) * g


def _twice_sigmoid_of_twice(x):
    return 1.0 + jnp.tanh(x)


def _twice_gelu_tanh(x):
    c = math.sqrt(2.0 / math.pi)
    return x * (1.0 + jnp.tanh(x * (c + (c * 0.044715) * (x * x))))


def _dot(a, b):
    return jnp.dot(a, b, preferred_element_type=F32)


def _dot_nt(a, b):
    return lax.dot_general(a, b, (((1,), (1,)), ((), ())), preferred_element_type=F32)


def _in_proj_body(x_ref, g_ref, wut_ref, wqkv0_ref, wqkvd_ref, u_ref, qkv0_ref, qkvd_ref, n_scr):
    tm = x_ref.shape[0]
    nf = _rms(x_ref[...], g_ref[...])
    n = nf.astype(BF16)
    for c in range(D_MODEL // LANES):
        n_scr[c] = nf[:, c * LANES:(c + 1) * LANES]
    ut = _dot_nt(wut_ref[...], n)
    for c in range(tm // SSM_CHUNK):
        u_ref[:, c] = ut[:, c * SSM_CHUNK:(c + 1) * SSM_CHUNK].reshape(
            SSM_GROUPS, SSM_GROUP_SIZE, SSM_CHUNK)
    qkv0_ref[...] = _dot(n, wqkv0_ref[...]).astype(BF16)
    rows = tm // MAX_DILATION
    n_perm = jnp.concatenate(
        [jnp.concatenate([n_scr[c, pl.ds(r, rows, stride=MAX_DILATION), :]
                          for c in range(D_MODEL // LANES)], axis=1)
         for r in range(MAX_DILATION)], axis=0)
    zp = _dot(n_perm.astype(BF16), wqkvd_ref[...]).astype(BF16)
    for r in range(MAX_DILATION):
        qkvd_ref[r] = zp[r * rows:(r + 1) * rows]


def _in_proj(x2, g, w_ut, w_qkv0, w_qkvd, bsz, seq, tm):
    tokens = bsz * seq
    nt = seq // tm
    cpt = tm // SSM_CHUNK
    n_chunks = seq // SSM_CHUNK
    wd = w_qkvd.shape[1]
    row = lambda b, i: (b * nt + i, 0)
    return pl.pallas_call(
        _in_proj_body,
        grid=(bsz, nt),
        in_specs=[
            pl.BlockSpec((tm, D_MODEL), row),
            _resident((1, D_MODEL)),
            _resident(w_ut.shape), _resident(w_qkv0.shape), _resident(w_qkvd.shape),
        ],
        out_specs=[
            pl.BlockSpec((SSM_GROUPS, cpt, SSM_GROUP_SIZE, SSM_CHUNK), lambda b, i: (0, i, b, 0)),
            pl.BlockSpec((tm, ATTN_WIDTH), row),
            pl.BlockSpec((None, MAX_DILATION, tm // MAX_DILATION, wd), lambda b, i: (b, 0, i, 0)),
        ],
        out_shape=[
            jax.ShapeDtypeStruct((SSM_GROUPS, n_chunks, bsz * SSM_GROUP_SIZE, SSM_CHUNK), F32),
            jax.ShapeDtypeStruct((tokens, ATTN_WIDTH), BF16),
            jax.ShapeDtypeStruct((bsz, MAX_DILATION, seq // MAX_DILATION, wd), BF16),
        ],
        scratch_shapes=[pltpu.VMEM((D_MODEL // LANES, tm, LANES), F32)],
        compiler_params=_params(2),
        name="in_proj",
    )(x2, g, w_ut, w_qkv0, w_qkvd)


def _ssm_body(u_ref, pk_ref, pq_ref, bc_ref, dtap_ref, y_ref,
              k_scr, t_scr, m_scr, nt_scr, z_scr, sp_scr, *, bsz):
    hh = SSM_GROUP_SIZE
    cc = u_ref.shape[1]
    rows = u_ref.shape[0] // hh
    n_chunks = rows // bsz
    half = 2 * SSM_STATE

    q_re, q_im = pq_ref[0], pq_ref[1]
    r_re, r_im = pk_ref[0, 1:cc + 1, :], pk_ref[1, 1:cc + 1, :]
    b_re, b_im, c_re, c_im = bc_ref[0], bc_ref[1], bc_ref[2], bc_ref[3]
    w_re, w_im = [], []
    for h in range(hh):
        rs = slice(h * cc, (h + 1) * cc)
        bre, bim, cre, cim = b_re[h:h + 1], b_im[h:h + 1], c_re[h:h + 1], c_im[h:h + 1]
        m_scr[rs, 0:half] = (q_re * bre - q_im * bim).astype(BF16)
        m_scr[rs, half:2 * half] = (q_re * bim + q_im * bre).astype(BF16)
        nt_scr[rs, 0:half] = (r_re * cre - r_im * cim).astype(BF16)
        nt_scr[rs, half:2 * half] = (-(r_im * cre + r_re * cim)).astype(BF16)
        w_re.append(c_re * bre - c_im * bim)
        w_im.append(c_re * bim + c_im * bre)
    taps_dot = lambda w, p: lax.dot_general(
        w, p, (((1,), (1,)), ((), ())), precision=lax.Precision.HIGHEST, preferred_element_type=F32)
    k_scr[...] = (taps_dot(jnp.concatenate(w_re, axis=0), pk_ref[0, 0:cc, :])
                  - taps_dot(jnp.concatenate(w_im, axis=0), pk_ref[1, 0:cc, :])
                  + dtap_ref[...])

    srow = lax.broadcasted_iota(jnp.int32, (cc, cc), 0)
    tcol = lax.broadcasted_iota(jnp.int32, (cc, cc), 1)
    causal = tcol >= srow

    x = jnp.concatenate([u_ref[pl.ds(h, rows, stride=hh), :].astype(BF16) for h in range(hh)],
                        axis=1)
    z_scr[...] = _dot(x, m_scr[...]).reshape(n_chunks, bsz, 2 * half)
    ar = pk_ref[0, cc:cc + 1, :]
    ai = pk_ref[1, cc:cc + 1, :]
    sr = jnp.zeros((bsz, half), F32)
    si = jnp.zeros((bsz, half), F32)
    for c in range(n_chunks):
        sp_scr[c, :, 0:half] = sr
        sp_scr[c, :, half:2 * half] = si
        zr = z_scr[c, :, 0:half]
        zi = z_scr[c, :, half:2 * half]
        sr, si = ar * sr - ai * si + zr, ar * si + ai * sr + zi
    sp = sp_scr[...].reshape(rows, 2 * half).astype(BF16)

    for p0 in range(0, hh, SSM_PANEL):
        cols = slice(p0 * cc, (p0 + SSM_PANEL) * cc)
        for h in range(hh):
            for hp in range(p0, p0 + SSM_PANEL):
                kb = jnp.broadcast_to(k_scr[h * hh + hp:h * hh + hp + 1, :], (cc, cc))
                blk = pltpu.roll(kb, 0, 1, stride=1, stride_axis=0)
                t_scr[h * cc:(h + 1) * cc, hp * cc:(hp + 1) * cc] = (
                    jnp.where(causal, blk, 0.0).astype(BF16))
        y = _dot(x, t_scr[:, cols]) + _dot_nt(sp, nt_scr[cols, :])
        for hp in range(p0, p0 + SSM_PANEL):
            y_ref[pl.ds(hp, rows, stride=hh), :] = y[:, (hp - p0) * cc:(hp - p0 + 1) * cc]


def _ssm(u4, pk, pq, bc, dtap, bsz):
    hh = SSM_GROUP_SIZE
    u3 = u4.reshape(u4.shape[0], -1, u4.shape[-1])
    _, rows16, cc = u3.shape
    n_chunks = rows16 // hh // bsz
    width = hh * cc
    half = 2 * SSM_STATE
    blk = pl.BlockSpec((None, rows16, cc), lambda g: (g, 0, 0))
    per_group = lambda a: pl.BlockSpec((None,) + a.shape[1:], lambda g: (g,) + (0,) * (a.ndim - 1))
    return pl.pallas_call(
        functools.partial(_ssm_body, bsz=bsz),
        grid=(SSM_GROUPS,),
        in_specs=[blk, per_group(pk), per_group(pq), per_group(bc), per_group(dtap)],
        out_specs=blk,
        out_shape=jax.ShapeDtypeStruct(u3.shape, F32),
        scratch_shapes=[
            pltpu.VMEM((hh * hh, cc), F32),
            pltpu.VMEM((width, width), BF16),
            pltpu.VMEM((width, 2 * half), BF16),
            pltpu.VMEM((width, 2 * half), BF16),
            pltpu.VMEM((n_chunks, bsz, 2 * half), F32),
            pltpu.VMEM((n_chunks, bsz, 2 * half), F32),
        ],
        compiler_params=_params(1),
        name="ssm",
    )(u3, pk, pq, bc, dtap).reshape(u4.shape)


def _ssm_operators(lam_re, lam_im, log_dt, b_re, b_im, c_re, c_im, d_skip):
    cc, pp, hh = SSM_CHUNK, SSM_STATE, SSM_GROUP_SIZE
    lr, li = lam_re.astype(F32), lam_im.astype(F32)
    dt = jnp.exp(log_dt.astype(F32))[:, None]
    mag = jnp.exp(lr * dt)
    a_re, a_im = mag * jnp.cos(li * dt), mag * jnp.sin(li * dt)
    nr, ni = a_re - 1.0, a_im
    den = lr * lr + li * li
    coef_re = (nr * lr + ni * li) / den
    coef_im = (ni * lr - nr * li) / den
    br_, bi_ = b_re.astype(F32), b_im.astype(F32)
    bb_re = coef_re[..., None] * br_ - coef_im[..., None] * bi_
    bb_im = coef_re[..., None] * bi_ + coef_im[..., None] * br_
    cr, ci = c_re.astype(F32), c_im.astype(F32)
    dd = d_skip.astype(F32)

    lane_pad = lambda v: jnp.pad(v, [(0, 0)] * (v.ndim - 1) + [(0, pp)])
    log_re, log_im = lane_pad(lr * dt), lane_pad(li * dt)
    live = (jnp.arange(2 * pp) < pp).astype(F32)

    def powers(ks):
        mag_k = jnp.exp(ks[None, :, None] * log_re[:, None, :]) * live[None, None, :]
        ang_k = ks[None, :, None] * log_im[:, None, :]
        return jnp.stack([mag_k * jnp.cos(ang_k), mag_k * jnp.sin(ang_k)], axis=1)

    pk = powers(jnp.arange(cc + 8, dtype=F32))
    pq = powers(cc - 1 - jnp.arange(cc, dtype=F32))
    bc = jnp.stack([lane_pad(bb_re.transpose(0, 2, 1)), lane_pad(bb_im.transpose(0, 2, 1)),
                    lane_pad(cr), lane_pad(ci)], axis=1)
    pair = jnp.arange(hh * hh)
    on_diag = jnp.logical_and((pair // hh == pair % hh)[:, None], (jnp.arange(cc) == 0)[None, :])
    dtap = jnp.repeat(dd, hh, axis=1)[:, :, None] * on_diag.astype(F32)[None]
    return pk, pq, bc, dtap


def _attn_tile(batches, pos_of, s_scr, p_scr):
    w, e, nh = ATTN_BLOCK, ATTN_HEAD_DIM, ATTN_HEADS_PER_GROUP
    lane = lax.broadcasted_iota(jnp.int32, (w, 2 * w), 1)
    pkey = pos_of(lane & (w - 1))
    pqry = pos_of(lax.broadcasted_iota(jnp.int32, (w, 2 * w), 0))
    mask = jnp.logical_or(jnp.logical_and(lane < w, pkey >= pqry),
                          jnp.logical_and(lane >= w, pkey <= pqry))
    head_of_lane = lane // e
    scale = e ** -0.5
    slot0 = [0]
    for blocks in batches:
        slot0.append(slot0[-1] + len(blocks) * nh)

    def scores(k):
        for bi, (q, kcat, _, has_prev, _) in enumerate(batches[k]):
            qs = q * scale
            qm = jnp.concatenate([jnp.where(head_of_lane == h, qs, 0) for h in range(nh)], axis=0)
            s = _dot_nt(qm, kcat).reshape(nh, w, 2 * w)
            if has_prev is not None:
                s = jnp.where(jnp.logical_or(lane >= w, has_prev)[None], s, NEG_INF)
            s_scr[slot0[k] + bi * nh:slot0[k] + (bi + 1) * nh] = s

    def softmax(k):
        s_all = jnp.where(mask[None], s_scr[slot0[k]:slot0[k + 1]], NEG_INF)
        mx = jnp.max(s_all, axis=-1, keepdims=True)
        p = jnp.exp(s_all - mx)
        den = jnp.sum(p, axis=-1, keepdims=True)
        p_scr[slot0[k]:slot0[k + 1]] = p.astype(BF16)
        return mx, den

    def values(k, mx, den):
        for bi, (_, _, vcat, _, write) in enumerate(batches[k]):
            lo = slot0[k] + bi * nh
            pv = _dot(p_scr[lo:lo + nh].reshape(nh * w, 2 * w), vcat).reshape(nh, w, nh * e)
            wide = lambda st: jnp.concatenate(
                [jnp.broadcast_to(st[bi * nh + h], (w, e)) for h in range(nh)], axis=1)
            den_t = wide(den)
            num = jnp.concatenate([pv[h][:, h * e:(h + 1) * e] for h in range(nh)], axis=1)
            write(num * (1.0 / den_t), wide(mx) + jnp.log(den_t))

    scores(0)
    for k in range(len(batches)):
        if k + 1 < len(batches):
            scores(k + 1)
        values(k, *softmax(k))


def _dil_attn_nat_body(q_ref, kc_ref, kh_ref, vc_ref, vh_ref, o_ref, l_ref, s_scr, p_scr):
    i = pl.program_id(1)
    w = ATTN_BLOCK
    pos_of = lambda idx: idx
    n_blk = q_ref.shape[0] // w
    batches = []
    for j0 in range(0, n_blk, ATTN_BATCH):
        blocks = []
        for j in range(j0, j0 + ATTN_BATCH):
            rs = slice(j * w, (j + 1) * w)
            if j == 0:
                kcat = jnp.concatenate([kh_ref[...], kc_ref[rs, :]], axis=0)
                vcat = jnp.concatenate([vh_ref[...], vc_ref[rs, :]], axis=0)
                has_prev = i > 0
            else:
                kcat, vcat, has_prev = kc_ref[(j - 1) * w:(j + 1) * w, :], vc_ref[(j - 1) * w:(j + 1) * w, :], None

            def write(o, lse, rs=rs):
                for half in range(ATTN_GROUP_WIDTH // LANES):
                    ls = slice(half * LANES, (half + 1) * LANES)
                    o_ref[half, rs, :] = o[:, ls]
                    l_ref[half, rs, :] = lse[:, ls]

            blocks.append((q_ref[rs, :], kcat, vcat, has_prev, write))
        batches.append(blocks)
    _attn_tile(batches, pos_of, s_scr, p_scr)


def _dil_attn_res_body(q_ref, kc_ref, kh_ref, vc_ref, vh_ref, o_ref, l_ref, s_scr, p_scr, *, dilation):
    i = pl.program_id(1)
    w = ATTN_BLOCK
    nq = MAX_DILATION // dilation
    rr = w // nq
    shift = rr.bit_length() - 1
    pos_of = lambda idx: nq * (idx & (rr - 1)) + (idx >> shift)

    def slabs(ref, r, lo):
        return [ref[q * dilation + r, lo:lo + rr, :] for q in range(nq)]

    todo = [(r, b) for r in range(dilation) for b in range(nq)]
    batches = []
    for j0 in range(0, len(todo), ATTN_BATCH):
        blocks = []
        for r, b in todo[j0:j0 + ATTN_BATCH]:
            if b == 0:
                kprev = [kh_ref[q * dilation + r] for q in range(nq)]
                vprev = [vh_ref[q * dilation + r] for q in range(nq)]
                has_prev = i > 0
            else:
                kprev, vprev, has_prev = slabs(kc_ref, r, (b - 1) * rr), slabs(vc_ref, r, (b - 1) * rr), None
            kcat = jnp.concatenate(kprev + slabs(kc_ref, r, b * rr), axis=0)
            vcat = jnp.concatenate(vprev + slabs(vc_ref, r, b * rr), axis=0)
            q = jnp.concatenate(slabs(q_ref, r, b * rr), axis=0)

            def write(o, lse, r=r, b=b):
                for qq in range(nq):
                    dst = pl.ds(MAX_DILATION * b * rr + qq * dilation + r, rr, stride=MAX_DILATION)
                    for half in range(ATTN_GROUP_WIDTH // LANES):
                        ls = slice(half * LANES, (half + 1) * LANES)
                        o_ref[half, dst, :] = o[qq * rr:(qq + 1) * rr, ls]
                        l_ref[half, dst, :] = lse[qq * rr:(qq + 1) * rr, ls]

            blocks.append((q, kcat, vcat, has_prev, write))
        batches.append(blocks)
    _attn_tile(batches, pos_of, s_scr, p_scr)


def _dil_attn(qkv, col0, dilation, bsz, seq):
    gw, w = ATTN_GROUP_WIDTH, ATTN_BLOCK
    nt = seq // ATTN_TILE
    n_sc = ATTN_TILE // w * ATTN_HEADS_PER_GROUP
    scratch = [pltpu.VMEM((n_sc, w, 2 * w), F32), pltpu.VMEM((n_sc, w, 2 * w), BF16)]
    out_spec = pl.BlockSpec((gw // LANES, ATTN_TILE, LANES), lambda b, i: (0, b * nt + i, 0))
    out_shape = [jax.ShapeDtypeStruct((gw // LANES, bsz * seq, LANES), F32)] * 2
    if dilation == 1:
        bpt = ATTN_TILE // w
        cur = lambda c: pl.BlockSpec((ATTN_TILE, gw), lambda b, i: (b * nt + i, c))
        halo = lambda c: pl.BlockSpec((w, gw), lambda b, i: (jnp.maximum((b * nt + i) * bpt - 1, 0), c))
        body = _dil_attn_nat_body
    else:
        rr = w // (MAX_DILATION // dilation)
        bpt = w // rr
        cur = lambda c: pl.BlockSpec((None, MAX_DILATION, w, gw), lambda b, i: (b, 0, i, c))
        halo = lambda c: pl.BlockSpec((None, MAX_DILATION, rr, gw),
                                      lambda b, i: (b, 0, jnp.maximum(i * bpt - 1, 0), c))
        body = functools.partial(_dil_attn_res_body, dilation=dilation)
    return pl.pallas_call(
        body,
        grid=(bsz, nt),
        in_specs=[cur(col0), cur(col0 + 1), halo(col0 + 1), cur(col0 + 2), halo(col0 + 2)],
        out_specs=[out_spec, out_spec],
        out_shape=out_shape,
        scratch_shapes=scratch,
        compiler_params=_params(2),
        name=f"dil_attn_d{dilation}",
    )(qkv, qkv, qkv, qkv, qkv)


def _mem_kv_body(mem_ref, g_ref, w_ref, k_ref, v_ref):
    n = _rms(mem_ref[...], g_ref[...]).astype(BF16)
    kv = _dot(n, w_ref[...])
    k_ref[...] = kv[:, :MEM_WIDTH].astype(BF16)
    v_ref[...] = kv[:, MEM_WIDTH:].astype(BF16)


def _mem_kv(mem, g, w_kv):
    bsz, mem_len, _ = mem.shape
    blk = pl.BlockSpec((None, mem_len, MEM_WIDTH), lambda b: (b, 0, 0))
    return pl.pallas_call(
        _mem_kv_body,
        grid=(bsz,),
        in_specs=[pl.BlockSpec((None, mem_len, D_MODEL), lambda b: (b, 0, 0)),
                  _resident((1, D_MODEL)), _resident(w_kv.shape)],
        out_specs=[blk, blk],
        out_shape=[jax.ShapeDtypeStruct((bsz, mem_len, MEM_WIDTH), BF16)] * 2,
        compiler_params=_params(1),
        name="mem_kv",
    )(mem, g, w_kv)


def _merge_body(y_ref, o0_ref, o1_ref, o2_ref, l0_ref, l1_ref, l2_ref, mk_ref, mv_ref, x_ref,
                g1_ref, wmq_ref, wg_ref, wglut_ref, bglu_ref, wssm_ref, wattn_ref, wmem_ref, wo_ref,
                bgate_ref, h_ref):
    n = _rms(x_ref[...], g1_ref[...]).astype(BF16)

    yt = jnp.concatenate([y_ref[:, c].reshape(SSM_WIDTH, SSM_CHUNK)
                          for c in range(y_ref.shape[1])], axis=1)
    gy = _twice_gelu_tanh(yt)
    glu = _dot(wglut_ref[...], gy.astype(BF16)) + bglu_ref[...]
    y2 = (gy * _twice_sigmoid_of_twice(glu)).T.astype(BF16)
    br_ssm = _dot(y2, wssm_ref[...])

    halves = []
    for half in range(ATTN_GROUP_WIDTH // LANES):
        l0, l1, l2 = l0_ref[half], l1_ref[half], l2_ref[half]
        mx = jnp.maximum(jnp.maximum(l0, l1), l2)
        e0, e1, e2 = jnp.exp(l0 - mx), jnp.exp(l1 - mx), jnp.exp(l2 - mx)
        halves.append((e0 * o0_ref[half] + e1 * o1_ref[half] + e2 * o2_ref[half]) / (e0 + e1 + e2))
    br_attn = _dot(jnp.concatenate(halves, axis=1).astype(BF16), wattn_ref[...])

    e = MEM_HEAD_DIM
    scale = e ** -0.5
    mq = _dot(n, wmq_ref[...]).astype(BF16)
    heads = []
    for h in range(MEM_HEADS):
        sl = slice(h * e, (h + 1) * e)
        s = _dot_nt(mq[:, sl], mk_ref[:, sl]) * scale
        p = jnp.exp(s - jnp.max(s, axis=-1, keepdims=True))
        inv = 1.0 / jnp.sum(p, axis=-1, keepdims=True)
        heads.append(_dot(p.astype(BF16), mv_ref[:, sl]) * inv)
    br_mem = _dot(jnp.concatenate(heads, axis=1).astype(BF16), wmem_ref[...])

    d = D_MODEL
    gate = lambda k: _twice_sigmoid_of_twice(
        _dot(n, wg_ref[:, k * d:(k + 1) * d]) + bgate_ref[:, k * d:(k + 1) * d])
    merged = gate(0) * br_ssm + gate(1) * br_attn + gate(2) * br_mem
    h_ref[...] = x_ref[...] + _dot(merged.astype(BF16), wo_ref[...])


def _merge(y3, os, ls, mk, mv, x2, weights, bsz, seq, tm):
    nt = seq // tm
    cpt = tm // SSM_CHUNK
    mem_len = mk.shape[1]
    row = lambda b, i: (b * nt + i, 0)
    tok = lambda width: pl.BlockSpec((tm, width), row)
    per_batch = pl.BlockSpec((None, mem_len, MEM_WIDTH), lambda b, i: (b, 0, 0))
    return pl.pallas_call(
        _merge_body,
        grid=(bsz, nt),
        in_specs=[pl.BlockSpec((SSM_GROUPS, cpt, SSM_GROUP_SIZE, SSM_CHUNK), lambda b, i: (0, i, b, 0))]
        + [pl.BlockSpec((ATTN_GROUP_WIDTH // LANES, tm, LANES), lambda b, i: (0, b * nt + i, 0))] * 6
        + [per_batch, per_batch, tok(D_MODEL)]
        + [_resident(w.shape) for w in weights],
        out_specs=tok(D_MODEL),
        out_shape=jax.ShapeDtypeStruct((bsz * seq, D_MODEL), F32),
        compiler_params=_params(2),
        name="merge",
    )(y3, *os, *ls, mk, mv, x2, *weights)


def _mlp_body(h_ref, g2_ref, wup_ref, wdown_ref, gf_ref, out_ref, *, final_norm, ff_chunk):
    h = h_ref[...]
    n2 = _rms(h, g2_ref[...]).astype(BF16)
    acc = h
    for c in range(0, D_FF, ff_chunk):
        a = jnp.maximum(_dot(n2, wup_ref[:, c:c + ff_chunk]), 0.0)
        acc = acc + _dot((a * a).astype(BF16), wdown_ref[c:c + ff_chunk, :])
    out_ref[...] = _rms(acc, gf_ref[...]) if final_norm else acc


def _mlp(h, g2, w_up, w_down, gf, final_norm, tm):
    tokens = h.shape[0]
    row = lambda i: (i, 0)
    return pl.pallas_call(
        functools.partial(_mlp_body, final_norm=final_norm, ff_chunk=D_MODEL),
        grid=(tokens // tm,),
        in_specs=[pl.BlockSpec((tm, D_MODEL), row), _resident((1, D_MODEL)),
                  _resident(w_up.shape), _resident(w_down.shape), _resident((1, D_MODEL))],
        out_specs=pl.BlockSpec((tm, D_MODEL), row),
        out_shape=jax.ShapeDtypeStruct((tokens, D_MODEL), F32),
        compiler_params=_params(1),
        name="mlp",
    )(h, g2, w_up, w_down, gf)


def kernel(x, mem, norm1_g, mem_norm_g, w_in, b_gate, ssm_lambda_re, ssm_lambda_im, ssm_log_dt, ssm_b_re, ssm_b_im, ssm_c_re, ssm_c_im, ssm_d, w_glu, b_glu, w_ssm_br, w_attn_br, w_mem_kv, w_mem_br, w_o, norm2_g, w_up, w_down, final_g):
    bsz, seq, d_model = x.shape
    depth = w_in.shape[0]
    assert d_model == D_MODEL and seq % ATTN_TILE == 0
    assert tuple(w // d for w, d in ATTN_PATTERNS) == (ATTN_BLOCK,) * 3
    assert tuple(d for _, d in ATTN_PATTERNS) == (1, 4, 16)
    tm = TOKEN_TILE
    h = x.reshape(bsz * seq, D_MODEL)
    row = lambda v: v.reshape(1, -1).astype(F32)
    aw, gw = ATTN_WIDTH, ATTN_GROUP_WIDTH
    o_q, o_k, o_v = SSM_WIDTH, SSM_WIDTH + aw, SSM_WIDTH + 2 * aw
    o_mq = SSM_WIDTH + 3 * aw
    o_g = o_mq + MEM_WIDTH
    for i in range(depth):
        wi = w_in[i]
        cols = lambda g: [wi[:, o + g * gw:o + (g + 1) * gw] for o in (o_q, o_k, o_v)]
        w_qkv0 = jnp.concatenate(cols(0), axis=1).astype(BF16)
        w_qkvd = jnp.concatenate(cols(1) + cols(2), axis=1).astype(BF16)
        u3, qkv0, qkvd = _in_proj(
            h, row(norm1_g[i]), wi[:, :SSM_WIDTH].T.astype(BF16), w_qkv0, w_qkvd, bsz, seq, tm)

        ops = _ssm_operators(ssm_lambda_re[i], ssm_lambda_im[i], ssm_log_dt[i], ssm_b_re[i],
                             ssm_b_im[i], ssm_c_re[i], ssm_c_im[i], ssm_d[i])
        y3 = _ssm(u3, *ops, bsz)

        outs = [_dil_attn(qkv0, 0, 1, bsz, seq), _dil_attn(qkvd, 0, 4, bsz, seq),
                _dil_attn(qkvd, 3, 16, bsz, seq)]
        os, ls = [o for o, _ in outs], [l for _, l in outs]

        mk, mv = _mem_kv(mem, row(mem_norm_g[i]), w_mem_kv[i].astype(BF16))

        merge_weights = [
            row(norm1_g[i]), wi[:, o_mq:o_g].astype(BF16), (0.5 * wi[:, o_g:]).astype(BF16),
            (0.25 * w_glu[i]).T.astype(BF16), (0.5 * b_glu[i]).reshape(-1, 1).astype(F32),
            (0.125 * w_ssm_br[i]).astype(BF16), (0.5 * w_attn_br[i]).astype(BF16),
            (0.5 * w_mem_br[i]).astype(BF16), w_o[i].astype(BF16), row(0.5 * b_gate[i])]
        h = _merge(y3, os, ls, mk, mv, h, merge_weights, bsz, seq, tm)

        h = _mlp(h, row(norm2_g[i]), w_up[i].astype(BF16), w_down[i].astype(BF16), row(final_g),
                 final_norm=(i == depth - 1), tm=MLP_TOKEN_TILE)
    return h.reshape(bsz, seq, D_MODEL)
```
